```python
import math
import numpy as np
import jax
import jax.numpy as jnp
from jax import lax

D_MODEL = 1024
BATCH = 32
SEQ = 2048
DEPTH = 2

GRID_W = 64
CTX_LEN = 256
N_BRANCH = 3
BRANCH_W = D_MODEL // 2
RMS_EPS = 1e-6
NEG = -1e30
MLA_HEADS = BRANCH_W // 64
MLA_NOPE = 64
MLA_ROPE = 32
MLA_QK = MLA_NOPE + MLA_ROPE
MLA_V = BRANCH_W // MLA_HEADS
KV_LORA = 4 * MLA_V
Q_LORA = 3 * KV_LORA
Q_BLOCK = 128
ROPE_THETA = 10000.0
NAT_HEADS = BRANCH_W // 64
NAT_DIM = 64
WIN_H = 8
WIN_W = 16
COL_BLOCK = 16
COL_BAND = 2 * WIN_W
N_COL_BLOCKS = GRID_W // COL_BLOCK
S5_W = BRANCH_W
S5_GROUP = 16
S5_GROUPS = S5_W // S5_GROUP
S5_STATE = 64
N_GROUPS = 4
EXPERTS_PER_GROUP = 8
N_EXPERTS = N_GROUPS * EXPERTS_PER_GROUP
EXPERT_HIDDEN = D_MODEL // 2
TOP_K = 2
MOE_BLOCK = 256
C_KV = 0
C_KR = C_KV + KV_LORA
C_NK = C_KR + MLA_ROPE
C_NV = C_NK + NAT_HEADS * NAT_DIM
C_S5 = C_NV + NAT_HEADS * NAT_DIM
C_CTX_END = C_S5 + S5_W
C_QL = C_CTX_END
C_NQ = C_QL + Q_LORA
C_GATE = C_NQ + NAT_HEADS * NAT_DIM
N_IN = C_GATE + N_BRANCH * D_MODEL

kernel_name = 'hybrid_mla_nat_s5_hmoe_dit'


def rmsnorm(x, g):
    xf = x.astype(jnp.float32)
    y = xf * lax.rsqrt(jnp.mean(xf * xf, axis=-1, keepdims=True) + RMS_EPS)
    return (y * g.astype(jnp.float32)).astype(x.dtype)


def modulate(x, g, shift, scale):
    return rmsnorm(x, g) * (1 + scale) + shift


def axial_rope_tables(L):
    t = jnp.arange(L)
    nf = MLA_ROPE // 4
    inv = ROPE_THETA ** (-jnp.arange(nf, dtype=jnp.float32) / nf)
    ang = jnp.concatenate([(t // GRID_W).astype(jnp.float32)[:, None] * inv,
                           (t % GRID_W).astype(jnp.float32)[:, None] * inv], axis=-1)
    return jnp.cos(ang), jnp.sin(ang)


def apply_axial_rope(x, cos, sin):
    B, L, H, R = x.shape
    nf = R // 4
    xr = x.astype(jnp.float32).reshape(B, L, H, 2, 2, nf)
    x1, x2 = xr[..., 0, :], xr[..., 1, :]
    cs = cos.reshape(L, 1, 2, nf)
    sn = sin.reshape(L, 1, 2, nf)
    out = jnp.stack([x1 * cs - x2 * sn, x1 * sn + x2 * cs], axis=-2)
    return out.reshape(B, L, H, R).astype(x.dtype)


def dense_attn(q, k, v):
    scale = q.shape[-1] ** -0.5
    s = jnp.einsum('bqhd,bkhd->bhqk', q, k).astype(jnp.float32) * scale
    p = jax.nn.softmax(s, axis=-1).astype(v.dtype)
    return jnp.einsum('bhqk,bkhe->bqhe', p, v)


def blocked_attn(q, k, v):
    B, L, H, dq = q.shape
    qb = q.reshape(B, L // Q_BLOCK, Q_BLOCK, H, dq).swapaxes(0, 1)
    ob = lax.map(lambda qi: dense_attn(qi, k, v), qb)
    return ob.swapaxes(0, 1).reshape(B, L, H * v.shape[-1])


def mla_keys(kv_lat, k_rope, w_ukv, g_kva, g_kn, rope):
    B, L, _ = kv_lat.shape
    kv = (rmsnorm(kv_lat, g_kva) @ w_ukv).reshape(B, L, MLA_HEADS, MLA_NOPE + MLA_V)
    k_nope, v = kv[..., :MLA_NOPE], kv[..., MLA_NOPE:]
    k_r = jnp.broadcast_to(k_rope[:, :, None, :], (B, L, MLA_HEADS, MLA_ROPE))
    k = rmsnorm(jnp.concatenate([k_nope, k_r], axis=-1), g_kn)
    if rope is not None:
        k = jnp.concatenate([k[..., :MLA_NOPE], apply_axial_rope(k[..., MLA_NOPE:], *rope)], axis=-1)
    return k, v


def mla_queries(q_lat, w_uq, g_qa, g_qn, rope):
    B, L, _ = q_lat.shape
    q = (rmsnorm(q_lat, g_qa) @ w_uq).reshape(B, L, MLA_HEADS, MLA_QK)
    q = rmsnorm(q, g_qn)
    if rope is not None:
        q = jnp.concatenate([q[..., :MLA_NOPE], apply_axial_rope(q[..., MLA_NOPE:], *rope)], axis=-1)
    return q


def nat_col_tables():
    cols = np.arange(GRID_W)
    cs = np.clip(cols - WIN_W // 2, 0, GRID_W - WIN_W)
    band_start = np.clip(np.arange(N_COL_BLOCKS) * COL_BLOCK - WIN_W // 2, 0, GRID_W - COL_BAND)
    band_idx = band_start[:, None] + np.arange(COL_BAND)
    qcol = cols.reshape(N_COL_BLOCKS, COL_BLOCK)
    kcol = band_idx[:, None, :]
    qs = cs[qcol][..., None]
    valid = (kcol >= qs) & (kcol < qs + WIN_W)
    dcol = np.clip(kcol - qcol[..., None] + WIN_W - 1, 0, 2 * WIN_W - 2)
    return band_idx, valid, dcol


def nat_latent(q, k, v, kc, vc, rpb, rows):
    B, L, H, d = q.shape
    kh = min(WIN_H, rows)
    band_idx, valid, dcol = nat_col_tables()
    qg = q.reshape(B, rows, GRID_W, H, d)
    kg = k.reshape(B, rows, GRID_W, H, d)
    vg = v.reshape(B, rows, GRID_W, H, d)
    rpb_c = jnp.where(valid, rpb.astype(jnp.float32)[:, :, dcol], NEG)
    scale = d ** -0.5
    n_lat = kh * COL_BAND

    def row_block(r):
        rs = jnp.clip(r - kh // 2, 0, rows - kh)
        krow = lax.dynamic_slice_in_dim(kg, rs, kh, axis=1)
        vrow = lax.dynamic_slice_in_dim(vg, rs, kh, axis=1)
        kb = krow[:, :, band_idx].transpose(0, 2, 1, 3, 4, 5).reshape(B, N_COL_BLOCKS, n_lat, H, d)
        vb = vrow[:, :, band_idx].transpose(0, 2, 1, 3, 4, 5).reshape(B, N_COL_BLOCKS, n_lat, H, d)
        qr = lax.dynamic_index_in_dim(qg, r, axis=1, keepdims=False).reshape(B, N_COL_BLOCKS, COL_BLOCK, H, d)
        drow = rs + jnp.arange(kh) - r + WIN_H - 1
        bias = rpb_c[:, drow].transpose(0, 2, 3, 1, 4).reshape(H, N_COL_BLOCKS, COL_BLOCK, n_lat)
        s_lat = jnp.einsum('bjqhd,bjkhd->bhjqk', qr, kb).astype(jnp.float32) * scale + bias[None]
        s_ctx = jnp.einsum('bjqhd,bkhd->bhjqk', qr, kc).astype(jnp.float32) * scale
        p = jax.nn.softmax(jnp.concatenate([s_lat, s_ctx], axis=-1), axis=-1).astype(v.dtype)
        o = (jnp.einsum('bhjqk,bjkhd->bjqhd', p[..., :n_lat], vb)
             + jnp.einsum('bhjqk,bkhd->bjqhd', p[..., n_lat:], vc))
        return o.reshape(B, GRID_W, H, d)

    out = lax.map(row_block, jnp.arange(rows))
    return out.transpose(1, 0, 2, 3, 4).reshape(B, L, H * d)


def s5_discretize(a_re, a_im, log_dt, b_re, b_im):
    lam_re = jnp.minimum(a_re.astype(jnp.float32), -1e-4)
    lam_im = a_im.astype(jnp.float32)
    dt = jnp.exp(log_dt.astype(jnp.float32))[:, None]
    mag = jnp.exp(lam_re * dt)
    ab_re = mag * jnp.cos(lam_im * dt)
    ab_im = mag * jnp.sin(lam_im * dt)
    den = lam_re * lam_re + lam_im * lam_im
    n_re = ab_re - 1.0
    f_re = (n_re * lam_re + ab_im * lam_im) / den
    f_im = (ab_im * lam_re - n_re * lam_im) / den
    br = b_re.astype(jnp.float32)
    bi = b_im.astype(jnp.float32)
    bb_re = f_re[..., None] * br - f_im[..., None] * bi
    bb_im = f_re[..., None] * bi + f_im[..., None] * br
    return ab_re, ab_im, bb_re, bb_im


def _complex_affine_combine(e1, e2):
    a1r, a1i, b1r, b1i = e1
    a2r, a2i, b2r, b2i = e2
    return (a1r * a2r - a1i * a2i, a1r * a2i + a1i * a2r,
            a2r * b1r - a2i * b1i + b2r, a2r * b1i + a2i * b1r + b2i)


def s5_scan(ab_re, ab_im, bu_re, bu_im, h0, reverse):
    T = bu_re.shape[0]
    if h0 is not None:
        i0 = T - 1 if reverse else 0
        h0_re, h0_im = h0
        bu_re = bu_re.at[i0].add(ab_re * h0_re - ab_im * h0_im)
        bu_im = bu_im.at[i0].add(ab_re * h0_im + ab_im * h0_re)
    a_re = jnp.broadcast_to(ab_re, (T, 1) + ab_re.shape)
    a_im = jnp.broadcast_to(ab_im, (T, 1) + ab_im.shape)
    _, _, h_re, h_im = lax.associative_scan(_complex_affine_combine, (a_re, a_im, bu_re, bu_im),
                                            reverse=reverse, axis=0)
    return h_re, h_im


def s5_branch(u, uc, a_re, a_im, log_dt, b_re, b_im, c_re, c_im, d_skip, w_glu, need_ctx):
    B, L, _ = u.shape
    Cx = uc.shape[1]
    ug = u.astype(jnp.float32).reshape(B, L, S5_GROUPS, S5_GROUP)
    ucg = uc.astype(jnp.float32).reshape(B, Cx, S5_GROUPS, S5_GROUP)
    d_g = d_skip.astype(jnp.float32).reshape(S5_GROUPS, S5_GROUP)
    y = ug * d_g
    yc = ucg * d_g if need_ctx else None
    for di, rev in ((0, False), (1, True)):
        ab_re, ab_im, bb_re, bb_im = s5_discretize(a_re[di], a_im[di], log_dt[di], b_re[di], b_im[di])
        cr = c_re[di].astype(jnp.float32)
        ci = c_im[di].astype(jnp.float32)

        def drive(v):
            return (jnp.einsum('btgn,gpn->tbgp', v, bb_re), jnp.einsum('btgn,gpn->tbgp', v, bb_im))

        def readout(hr, hi):
            return jnp.einsum('tbgp,gnp->btgn', hr, cr) - jnp.einsum('tbgp,gnp->btgn', hi, ci)

        hc_re, hc_im = s5_scan(ab_re, ab_im, *drive(ucg), None, rev)
        last = 0 if rev else Cx - 1
        h_re, h_im = s5_scan(ab_re, ab_im, *drive(ug), (hc_re[last], hc_im[last]), rev)
        y = y + readout(h_re, h_im)
        if need_ctx:
            yc = yc + readout(hc_re, hc_im)

    def glu(t):
        z = jax.nn.gelu(t.reshape(t.shape[0], t.shape[1], S5_W)).astype(u.dtype) @ w_glu
        return z[..., :S5_W] * jax.nn.sigmoid(z[..., S5_W:])

    return glu(y), (glu(yc) if need_ctx else None)


def merge_branches(gate_cols, outs, w_branch, w_out):
    g = jax.nn.sigmoid(gate_cols.reshape(gate_cols.shape[:-1] + (N_BRANCH, D_MODEL)))
    y = g[..., 0, :] * (outs[0] @ w_branch[0])
    for i in range(1, N_BRANCH):
        y = y + g[..., i, :] * (outs[i] @ w_branch[i])
    return y @ w_out


def hier_moe(h, w_group, b_group, w_expert, b_expert, w_up, w_down):
    T, D = h.shape
    g_logit = (h @ w_group).astype(jnp.float32) + b_group.astype(jnp.float32)
    grp = jnp.argmax(g_logit, axis=-1)
    p_grp = jnp.take_along_axis(jax.nn.softmax(g_logit, axis=-1), grp[:, None], axis=-1)[:, 0]
    e_logit = ((h @ w_expert).astype(jnp.float32) + b_expert.astype(jnp.float32)).reshape(
        T, N_GROUPS, EXPERTS_PER_GROUP)
    e_logit = jnp.take_along_axis(e_logit, grp[:, None, None], axis=1)[:, 0]
    top_v, top_i = lax.top_k(e_logit, TOP_K)
    gate = p_grp[:, None] * jax.nn.softmax(top_v, axis=-1)
    eid = grp[:, None] * EXPERTS_PER_GROUP + top_i
    S = T * TOP_K
    e_flat = eid.reshape(S)
    tok_flat = jnp.repeat(jnp.arange(T, dtype=jnp.int32), TOP_K)
    order = jnp.argsort(e_flat)
    e_s, tok_s, w_s = e_flat[order], tok_flat[order], gate.reshape(S)[order]
    counts = jnp.bincount(e_flat, length=N_EXPERTS)
    start = jnp.cumsum(counts) - counts
    padded = (counts + MOE_BLOCK - 1) // MOE_BLOCK * MOE_BLOCK
    p_end = jnp.cumsum(padded)
    p_start = p_end - padded
    dest = p_start[e_s] + jnp.arange(S) - start[e_s]
    n_blk = -(-(S + N_EXPERTS * (MOE_BLOCK - 1)) // MOE_BLOCK)
    buf_tok = jnp.full((n_blk * MOE_BLOCK,), T, jnp.int32).at[dest].set(tok_s)
    buf_w = jnp.zeros((n_blk * MOE_BLOCK,), jnp.float32).at[dest].set(w_s)
    blk_e = jnp.minimum(jnp.searchsorted(p_end, jnp.arange(n_blk) * MOE_BLOCK, side='right'), N_EXPERTS - 1)
    h_ext = jnp.concatenate([h, jnp.zeros((1, D), h.dtype)], axis=0)

    def expert_block(args):
        tb, wb, eb = args
        a = h_ext[tb] @ w_up[eb]
        hid = jax.nn.silu(a[:, :EXPERT_HIDDEN]) * a[:, EXPERT_HIDDEN:]
        return (hid @ w_down[eb]) * wb[:, None].astype(h.dtype)

    y = lax.map(expert_block, (buf_tok.reshape(n_blk, MOE_BLOCK), buf_w.reshape(n_blk, MOE_BLOCK), blk_e))
    out = jnp.zeros((T + 1, D), h.dtype).at[buf_tok].add(y.reshape(-1, D))
    return out[:T]


def setup_inputs(seed: int = 0) -> dict:
    key = jax.random.key(seed)
    ks = iter(jax.random.split(key, 48))

    def nrm(shape, s):
        return jax.random.normal(next(ks), shape, jnp.float32) * s

    def gain(shape):
        return 1.0 + nrm(shape, 0.02)

    s5_shape = (DEPTH, 2, S5_GROUPS, S5_STATE)
    return {
        'x': nrm((BATCH, SEQ, D_MODEL), 1.0),
        'c': nrm((BATCH, D_MODEL), 1.0),
        'ctx': nrm((BATCH, CTX_LEN, D_MODEL), 1.0),
        'c_ctx': nrm((D_MODEL,), 1.0),
        'w_ada': nrm((DEPTH, D_MODEL, 6 * D_MODEL), 0.5 * D_MODEL ** -0.5),
        'b_ada': nrm((DEPTH, 6 * D_MODEL), 0.02),
        'g_norm1': gain((DEPTH, D_MODEL)),
        'w_in': nrm((DEPTH, D_MODEL, N_IN), D_MODEL ** -0.5),
        'g_q_lat': gain((DEPTH, Q_LORA)),
        'w_uq': nrm((DEPTH, Q_LORA, MLA_HEADS * MLA_QK), Q_LORA ** -0.5),
        'g_kv_lat': gain((DEPTH, KV_LORA)),
        'w_ukv': nrm((DEPTH, KV_LORA, MLA_HEADS * (MLA_NOPE + MLA_V)), KV_LORA ** -0.5),
        'g_qn_mla': gain((DEPTH, MLA_QK)),
        'g_kn_mla': gain((DEPTH, MLA_QK)),
        'g_qn_nat': gain((DEPTH, NAT_DIM)),
        'g_kn_nat': gain((DEPTH, NAT_DIM)),
        'nat_rpb': nrm((DEPTH, NAT_HEADS, 2 * WIN_H - 1, 2 * WIN_W - 1), 0.1),
        's5_a_re': -0.5 + nrm(s5_shape, 0.01),
        's5_a_im': jnp.pi * jnp.arange(S5_STATE, dtype=jnp.float32) + nrm(s5_shape, 0.01),
        's5_log_dt': jax.random.uniform(next(ks), (DEPTH, 2, S5_GROUPS), jnp.float32,
                                        math.log(1e-3), math.log(1e-1)),
        's5_b_re': nrm(s5_shape + (S5_GROUP,), (2 * S5_GROUP) ** -0.5),
        's5_b_im': nrm(s5_shape + (S5_GROUP,), (2 * S5_GROUP) ** -0.5),
        's5_c_re': nrm((DEPTH, 2, S5_GROUPS, S5_GROUP, S5_STATE), 0.25),
        's5_c_im': nrm((DEPTH, 2, S5_GROUPS, S5_GROUP, S5_STATE), 0.25),
        's5_d': nrm((DEPTH, S5_W), 0.5),
        'w_glu': nrm((DEPTH, S5_W, 2 * S5_W), S5_W ** -0.5),
        'w_branch': nrm((DEPTH, N_BRANCH, BRANCH_W, D_MODEL), BRANCH_W ** -0.5),
        'w_out': nrm((DEPTH, D_MODEL, D_MODEL), D_MODEL ** -0.5),
        'g_norm2': gain((DEPTH, D_MODEL)),
        'w_group': nrm((DEPTH, D_MODEL, N_GROUPS), D_MODEL ** -0.5),
        'b_group': nrm((DEPTH, N_GROUPS), 0.01),
        'w_expert': nrm((DEPTH, D_MODEL, N_EXPERTS), D_MODEL ** -0.5),
        'b_expert': nrm((DEPTH, N_EXPERTS), 0.01),
        'w_up': nrm((DEPTH, N_EXPERTS, D_MODEL, 2 * EXPERT_HIDDEN), D_MODEL ** -0.5),
        'w_down': nrm((DEPTH, N_EXPERTS, EXPERT_HIDDEN, D_MODEL), EXPERT_HIDDEN ** -0.5),
    }


def reference(x, c, ctx, c_ctx, w_ada, b_ada, g_norm1, w_in, g_q_lat, w_uq, g_kv_lat, w_ukv,
              g_qn_mla, g_kn_mla, g_qn_nat, g_kn_nat, nat_rpb, s5_a_re, s5_a_im, s5_log_dt,
              s5_b_re, s5_b_im, s5_c_re, s5_c_im, s5_d, w_glu, w_branch, w_out, g_norm2,
              w_group, b_group, w_expert, b_expert, w_up, w_down):
    B, L, D = x.shape
    Cx = ctx.shape[1]
    rows = L // GRID_W
    rope = axial_rope_tables(L)
    s_c = jax.nn.silu(c)
    s_cc = jax.nn.silu(c_ctx)
    xc = ctx

    def nat_heads(t):
        return t.reshape(t.shape[0], t.shape[1], NAT_HEADS, NAT_DIM)

    for l in range(DEPTH):
        need_ctx = l < DEPTH - 1
        mod = (s_c @ w_ada[l] + b_ada[l])[:, None, :]
        sh1, sc1, gt1, sh2, sc2, gt2 = jnp.split(mod, 6, axis=-1)
        n_mc = 6 if need_ctx else 2
        modc = jnp.split(s_cc @ w_ada[l][:, :n_mc * D] + b_ada[l][:n_mc * D], n_mc)

        h = modulate(x, g_norm1[l], sh1, sc1)
        hc = modulate(xc, g_norm1[l], modc[0], modc[1])
        p = h @ w_in[l]
        pc = hc @ (w_in[l] if need_ctx else w_in[l][:, :C_CTX_END])

        k_a, v_a = mla_keys(p[..., C_KV:C_KR], p[..., C_KR:C_NK], w_ukv[l], g_kv_lat[l], g_kn_mla[l], rope)
        kc_a, vc_a = mla_keys(pc[..., C_KV:C_KR], pc[..., C_KR:C_NK], w_ukv[l], g_kv_lat[l], g_kn_mla[l], None)
        q_a = mla_queries(p[..., C_QL:C_NQ], w_uq[l], g_q_lat[l], g_qn_mla[l], rope)
        o_a = blocked_attn(q_a, jnp.concatenate([k_a, kc_a], axis=1), jnp.concatenate([v_a, vc_a], axis=1))

        q_b = rmsnorm(nat_heads(p[..., C_NQ:C_GATE]), g_qn_nat[l])
        k_b = rmsnorm(nat_heads(p[..., C_NK:C_NV]), g_kn_nat[l])
        v_b = nat_heads(p[..., C_NV:C_S5])
        kc_b = rmsnorm(nat_heads(pc[..., C_NK:C_NV]), g_kn_nat[l])
        vc_b = nat_heads(pc[..., C_NV:C_S5])
        o_b = nat_latent(q_b, k_b, v_b, kc_b, vc_b, nat_rpb[l], rows)

        o_c, oc_c = s5_branch(p[..., C_S5:C_CTX_END], pc[..., C_S5:C_CTX_END], s5_a_re[l], s5_a_im[l],
                              s5_log_dt[l], s5_b_re[l], s5_b_im[l], s5_c_re[l], s5_c_im[l], s5_d[l],
                              w_glu[l], need_ctx)

        y = merge_branches(p[..., C_GATE:], (o_a, o_b, o_c), w_branch[l], w_out[l])
        if need_ctx:
            qc_a = mla_queries(pc[..., C_QL:C_NQ], w_uq[l], g_q_lat[l], g_qn_mla[l], None)
            oc_a = dense_attn(qc_a, kc_a, vc_a).reshape(B, Cx, -1)
            qc_b = rmsnorm(nat_heads(pc[..., C_NQ:C_GATE]), g_qn_nat[l])
            oc_b = dense_attn(qc_b, kc_b, vc_b).reshape(B, Cx, -1)
            yc = merge_branches(pc[..., C_GATE:], (oc_a, oc_b, oc_c), w_branch[l], w_out[l])
            xc = xc + modc[2] * yc
        x = x + gt1 * y

        h2 = modulate(x, g_norm2[l], sh2, sc2).reshape(B * L, D)
        moe_w = (w_group[l], b_group[l], w_expert[l], b_expert[l], w_up[l], w_down[l])
        if need_ctx:
            h2c = modulate(xc, g_norm2[l], modc[3], modc[4]).reshape(B * Cx, D)
            f = hier_moe(jnp.concatenate([h2, h2c], axis=0), *moe_w)
            x = x + gt2 * f[:B * L].reshape(B, L, D)
            xc = xc + modc[5] * f[B * L:].reshape(B, Cx, D)
        else:
            x = x + gt2 * hier_moe(h2, *moe_w).reshape(B, L, D)
    return x
```

```python
import functools
import math

import numpy as np
import jax
import jax.numpy as jnp
from jax import lax
from jax.experimental import pallas as pl
from jax.experimental.pallas import tpu as pltpu

F32 = jnp.float32
BF16 = jnp.bfloat16
HIGHEST = lax.Precision.HIGHEST

D_MODEL = 1024
GRID_W = 64
N_BRANCH = 3
BRANCH_W = 512
RMS_EPS = 1e-6
NEG = -1e30
HEADS = 8
MLA_NOPE = 64
MLA_ROPE = 32
MLA_QK = 96
MLA_V = 64
KV_LORA = 256
Q_LORA = 768
ROPE_THETA = 10000.0
NAT_DIM = 64
WIN_H = 8
WIN_W = 16
S5_W = 512
S5_GROUP = 16
S5_GROUPS = 32
S5_STATE = 64
N_GROUPS = 4
EXPERTS_PER_GROUP = 8
N_EXPERTS = 32
EXPERT_HIDDEN = 512
TOP_K = 2

TM = 256
S5_CHUNK = 16
MOE_BLOCK = 256
HP = 128
VMEM_LIMIT = 56 * 1024 * 1024

P_KV = 0
P_KR = P_KV + KV_LORA
P_NK = P_KR + HP
P_NV = P_NK + 512
P_S5 = P_NV + 512
P_QL = P_S5 + 512
P_NQ = P_QL + Q_LORA
P_GATE = P_NQ + 512
P_END = P_GATE + N_BRANCH * D_MODEL

C_KV = 0
C_KR = C_KV + KV_LORA
C_NK = C_KR + MLA_ROPE
C_NV = C_NK + 512
C_S5 = C_NV + 512
C_QL = C_S5 + S5_W
C_NQ = C_QL + Q_LORA
C_GATE = C_NQ + 512


def _cparams(*sem):
    return pltpu.CompilerParams(dimension_semantics=sem, vmem_limit_bytes=VMEM_LIMIT)


def _const_spec(shape):
    nd = len(shape)
    return pl.BlockSpec(shape, lambda *_: (0,) * nd, pipeline_mode=pl.Buffered(1))


def _ada_kernel(c_ref, w_ref, b_ref, o_ref):
    c = c_ref[...]
    s = (c * jax.nn.sigmoid(c)).astype(BF16)
    o_ref[...] = jnp.dot(s, w_ref[...].astype(BF16), preferred_element_type=F32) + b_ref[...]


def _ada(c_all, w_ada, b_ada):
    rows, d = c_all.shape
    n = w_ada.shape[1]
    bn = 512
    return pl.pallas_call(
        _ada_kernel,
        out_shape=jax.ShapeDtypeStruct((rows, n), F32),
        grid=(n // bn,),
        in_specs=[pl.BlockSpec((rows, d), lambda i: (0, 0)),
                  pl.BlockSpec((d, bn), lambda i: (0, i)),
                  pl.BlockSpec((1, bn), lambda i: (0, i))],
        out_specs=pl.BlockSpec((rows, bn), lambda i: (0, i)),
        compiler_params=_cparams("arbitrary"),
        name="ada_mod",
    )(c_all, w_ada, b_ada.reshape(1, n))


def _rms(x, g):
    return x * lax.rsqrt(jnp.mean(x * x, axis=-1, keepdims=True) + RMS_EPS) * g


def _mla_head(xh, w1, w2):
    lane = lax.broadcasted_iota(jnp.int32, xh.shape, 1)
    sq = jnp.where(lane < MLA_QK, xh * xh, 0.0)
    inv = lax.rsqrt(jnp.sum(sq, axis=-1, keepdims=True) * (1.0 / MLA_QK) + RMS_EPS)
    return inv * (xh * w1 + pltpu.roll(xh, HP - MLA_ROPE, 1) * w2)


def _pair_norm(x2, g2):
    lane = lax.broadcasted_iota(jnp.int32, x2.shape, 1)
    lo = lane < NAT_DIM
    sq = x2 * x2
    s_lo = jnp.sum(jnp.where(lo, sq, 0.0), axis=-1, keepdims=True)
    s_hi = jnp.sum(jnp.where(lo, 0.0, sq), axis=-1, keepdims=True)
    inv = lax.rsqrt(jnp.where(lo, s_lo, s_hi) * (1.0 / NAT_DIM) + RMS_EPS)
    return x2 * inv * g2


def _front_kernel(x_ref, mod_ref, g1_ref, win_ref, gkv_ref, wukv_ref, gq_ref, wuq_ref,
                  kw1_ref, kw2_ref, qw1_ref, qw2_ref, gnq_ref, gnk_ref,
                  qm_ref, km_ref, vm_ref, qn_ref, kn_ref, vn_ref, u_ref, gate_ref):
    x = x_ref[0]
    sh = mod_ref[0, 0, 0:1, :]
    sc = mod_ref[0, 0, 1:2, :]
    h = (_rms(x, g1_ref[...]) * (1.0 + sc) + sh).astype(BF16)

    def proj(a, b):
        return jnp.dot(h, win_ref[:, a:b], preferred_element_type=F32)

    kvl = _rms(proj(P_KV, P_KR), gkv_ref[...]).astype(BF16)
    krp = proj(P_KR, P_NK)
    kw1 = kw1_ref[...]
    kw2 = kw2_ref[...]
    for hd in range(HEADS):
        kn = jnp.dot(kvl, wukv_ref[:, hd * HP:(hd + 1) * HP], preferred_element_type=F32)
        km_ref[0, :, hd * HP:(hd + 1) * HP] = _mla_head(kn + krp, kw1, kw2).astype(BF16)
    vm_ref[0] = jnp.dot(kvl, wukv_ref[:, HEADS * HP:], preferred_element_type=F32).astype(BF16)

    ql = _rms(proj(P_QL, P_NQ), gq_ref[...]).astype(BF16)
    qw1 = qw1_ref[...]
    qw2 = qw2_ref[...]
    for hd in range(HEADS):
        qh = jnp.dot(ql, wuq_ref[:, hd * HP:(hd + 1) * HP], preferred_element_type=F32)
        qm_ref[0, :, hd * HP:(hd + 1) * HP] = _mla_head(qh, qw1, qw2).astype(BF16)

    gnq = gnq_ref[...]
    gnk = gnk_ref[...]
    for c in range(4):
        kk = proj(P_NK + c * 128, P_NK + (c + 1) * 128)
        kn_ref[0, :, c * 128:(c + 1) * 128] = _pair_norm(kk, gnk).astype(BF16)
        qq = proj(P_NQ + c * 128, P_NQ + (c + 1) * 128)
        qn_ref[0, :, c * 128:(c + 1) * 128] = _pair_norm(qq, gnq).astype(BF16)
    vn_ref[0] = proj(P_NV, P_S5).astype(BF16)

    u_ref[0] = proj(P_S5, P_QL).astype(BF16)
    for c in range(N_BRANCH * D_MODEL // 512):
        gate_ref[0, :, c * 512:(c + 1) * 512] = jax.nn.sigmoid(
            proj(P_GATE + c * 512, P_GATE + (c + 1) * 512)).astype(BF16)


def _front(xs, mods, g1, w_in_p, gkv, wukv_p, gq, wuq_p, kw1, kw2, qw1, qw2, gnq, gnk):
    B, S, D = xs.shape
    nt = S // TM
    tok = lambda w: pl.BlockSpec((1, TM, w), lambda b, j: (b, j, 0))
    tab = pl.BlockSpec((TM, HP), lambda b, j: (j, 0))
    outs = [(HEADS * HP, BF16), (HEADS * HP, BF16), (HEADS * HP, BF16), (512, BF16), (512, BF16),
            (512, BF16), (S5_W, BF16), (N_BRANCH * D_MODEL, BF16)]
    return pl.pallas_call(
        _front_kernel,
        out_shape=[jax.ShapeDtypeStruct((B, S, w), dt) for w, dt in outs],
        grid=(B, nt),
        in_specs=[tok(D),
                  pl.BlockSpec((1, 1, 6, D), lambda b, j: (b, jnp.minimum(j, 1), 0, 0)),
                  _const_spec((1, D)), _const_spec(w_in_p.shape), _const_spec((1, KV_LORA)),
                  _const_spec(wukv_p.shape), _const_spec((1, Q_LORA)), _const_spec(wuq_p.shape),
                  tab, tab, tab, tab, _const_spec((1, 128)), _const_spec((1, 128))],
        out_specs=[tok(w) for w, _ in outs],
        compiler_params=_cparams("parallel", "arbitrary"),
        name="mixer_front",
    )(xs, mods, g1, w_in_p, gkv, wukv_p, gq, wuq_p, kw1, kw2, qw1, qw2, gnq, gnk)


def _nt(a, b):
    return lax.dot_general(a, b, (((1,), (1,)), ((), ())), preferred_element_type=F32)


def _softmax_pv(s, v):
    m = jnp.max(s, axis=-1, keepdims=True)
    p = jnp.exp(s - m)
    l = jnp.sum(p, axis=-1, keepdims=True)
    return jnp.dot(p.astype(BF16), v, preferred_element_type=F32) * (1.0 / l)


def _mla_kernel(q_ref, k_ref, v_ref, o_ref, *, seq, need_ctx):
    j = pl.program_id(2)

    def attend(nk):
        o = None
        for hh in range(2):
            sl = slice(hh * HP, (hh + 1) * HP)
            s = _nt(q_ref[0, :, sl], k_ref[0, :nk, sl])
            t = _softmax_pv(s, v_ref[0, :nk, sl])
            o = t if o is None else o + t
        o_ref[0] = o.astype(BF16)

    @pl.when(j == 0)
    def _():
        if need_ctx:
            attend(TM)
        else:
            o_ref[...] = jnp.zeros(o_ref.shape, BF16)

    @pl.when(j > 0)
    def _():
        attend(seq)


def _mla(qm, km, vm, need_ctx):
    B, S, _ = qm.shape
    nt = S // TM
    return pl.pallas_call(
        functools.partial(_mla_kernel, seq=S, need_ctx=need_ctx),
        out_shape=jax.ShapeDtypeStruct((B, S, BRANCH_W), BF16),
        grid=(B, HEADS // 2, nt),
        in_specs=[pl.BlockSpec((1, TM, 2 * HP), lambda b, h, j: (b, j, h)),
                  pl.BlockSpec((1, S, 2 * HP), lambda b, h, j: (b, 0, h)),
                  pl.BlockSpec((1, S, 2 * HP), lambda b, h, j: (b, 0, h))],
        out_specs=pl.BlockSpec((1, TM, 128), lambda b, h, j: (b, j, h)),
        compiler_params=_cparams("parallel", "parallel", "arbitrary"),
        name="mla_attn",
    )(qm, km, vm)


def _nat_kernel(q_ref, k_ref, v_ref, bias_ref, o_ref, *, rows, need_ctx):
    lane = lax.broadcasted_iota(jnp.int32, (1, 128), 1)
    lo = lane < NAT_DIM
    masks = (lo, jnp.logical_not(lo))
    kc = k_ref[0, 0:TM, :]
    vc = v_ref[0, 0:TM, :]
    vcm = [jnp.where(m, vc, jnp.zeros_like(vc)) for m in masks]

    if need_ctx:
        qc = q_ref[0, 0:TM, :]
        o = None
        for hh in range(2):
            s = _nt(jnp.where(masks[hh], qc, jnp.zeros_like(qc)), kc)
            t = _softmax_pv(s, vcm[hh])
            o = t if o is None else o + t
        o_ref[0, 0:TM, :] = o.astype(BF16)
    else:
        o_ref[0, 0:TM, :] = jnp.zeros((TM, 128), BF16)

    def row(r, carry):
        rs = jnp.clip(r - WIN_H // 2, 0, rows - WIN_H)
        cls = r - rs
        q0 = pl.multiple_of(TM + r * GRID_W, GRID_W)
        k0 = pl.multiple_of(TM + rs * GRID_W, GRID_W)
        q = q_ref[0, pl.ds(q0, GRID_W), :]
        kw = k_ref[0, pl.ds(k0, WIN_H * GRID_W), :]
        vw = v_ref[0, pl.ds(k0, WIN_H * GRID_W), :]
        o = None
        for hh in range(2):
            qm = jnp.where(masks[hh], q, jnp.zeros_like(q))
            s_lat = _nt(qm, kw) + bias_ref[0, hh, cls]
            s_ctx = _nt(qm, kc)
            m = jnp.maximum(jnp.max(s_lat, axis=-1, keepdims=True), jnp.max(s_ctx, axis=-1, keepdims=True))
            p_lat = jnp.exp(s_lat - m)
            p_ctx = jnp.exp(s_ctx - m)
            l = jnp.sum(p_lat, axis=-1, keepdims=True) + jnp.sum(p_ctx, axis=-1, keepdims=True)
            acc = (jnp.dot(p_lat.astype(BF16), jnp.where(masks[hh], vw, jnp.zeros_like(vw)),
                           preferred_element_type=F32)
                   + jnp.dot(p_ctx.astype(BF16), vcm[hh], preferred_element_type=F32))
            t = acc * (1.0 / l)
            o = t if o is None else o + t
        o_ref[0, pl.ds(q0, GRID_W), :] = o.astype(BF16)
        return carry

    lax.fori_loop(0, rows, row, 0)


def _nat(qn, kn, vn, bias, need_ctx):
    B, S, _ = qn.shape
    rows = (S - TM) // GRID_W
    seq = pl.BlockSpec((1, S, 128), lambda h, b: (b, 0, h))
    return pl.pallas_call(
        functools.partial(_nat_kernel, rows=rows, need_ctx=need_ctx),
        out_shape=jax.ShapeDtypeStruct((B, S, BRANCH_W), BF16),
        grid=(HEADS // 2, B),
        in_specs=[seq, seq, seq,
                  pl.BlockSpec((1, 2, WIN_H, GRID_W, WIN_H * GRID_W), lambda h, b: (h, 0, 0, 0, 0))],
        out_specs=seq,
        compiler_params=_cparams("parallel", "arbitrary"),
        name="nat_attn",
    )(qn, kn, vn, bias)


def _nat_bias_table(rpb):
    cols = np.arange(GRID_W)
    cs = np.clip(cols - WIN_W // 2, 0, GRID_W - WIN_W)
    kcol = cols[None, :]
    qcol = cols[:, None]
    valid = (kcol >= cs[:, None]) & (kcol < cs[:, None] + WIN_W)
    dcol = np.clip(kcol - qcol + WIN_W - 1, 0, 2 * WIN_W - 2)
    i = np.arange(WIN_H)
    cls = np.arange(WIN_H)
    drow = i[None, :] - cls[:, None] + WIN_H - 1
    t = rpb.astype(F32)[:, drow]
    t = t[:, :, :, dcol]
    t = jnp.where(valid[None, None, None], t, NEG)
    t = t.transpose(0, 1, 3, 2, 4).reshape(HEADS, WIN_H, GRID_W, WIN_H * GRID_W)
    return t.reshape(HEADS // 2, 2, WIN_H, GRID_W, WIN_H * GRID_W)


def _s5_kernel(u_ref, mt_ref, f_ref, e_ref, a_ref, y_ref, s_scr, hin_scr, *, batch, n_ctx, n_all):
    u = u_ref[0]
    for d in range(2):
        s_scr[d] = jnp.dot(u, f_ref[d, 0], preferred_element_type=F32)

    def make_step(d):
        a1 = a_ref[d, 0, 0:1, :]
        a2 = a_ref[d, 0, 1:2, :]
        a3 = a_ref[d, 0, 2:3, :]

        def step(c, hh):
            h, hs = hh
            r0 = pl.multiple_of(c * batch, batch)
            hin_scr[d, pl.ds(r0, batch), :] = h
            s = s_scr[d, pl.ds(r0, batch), :]
            return (h * a1 + hs * a2 + s[:, :128], hs * a1 + h * a3 + s[:, 128:])
        return step

    z = jnp.zeros((batch, 128), F32)
    lax.fori_loop(0, n_all, make_step(0), (z, z))
    rstep = make_step(1)
    hh = lax.fori_loop(0, n_ctx, lambda i, c: rstep(n_ctx - 1 - i, c), (z, z))
    lax.fori_loop(0, n_all - n_ctx, lambda i, c: rstep(n_all - 1 - i, c), hh)

    y = jnp.dot(u, mt_ref[0], preferred_element_type=F32)
    for d in range(2):
        y = y + jnp.dot(hin_scr[d].astype(BF16), e_ref[d, 0], preferred_element_type=F32)
    y_ref[0] = y.astype(BF16)


def _s5(u, mats):
    mt, fcat, e, a = mats
    B, S, _ = u.shape
    C = S // S5_CHUNK
    ug = u.reshape(B, C, S5_CHUNK, S5_GROUPS, S5_GROUP).transpose(3, 1, 0, 2, 4).reshape(
        S5_GROUPS, C * B, S5_CHUNK * S5_GROUP)
    w = S5_CHUNK * S5_GROUP
    yg = pl.pallas_call(
        functools.partial(_s5_kernel, batch=B, n_ctx=TM // S5_CHUNK, n_all=C),
        out_shape=jax.ShapeDtypeStruct((S5_GROUPS, C * B, w), BF16),
        grid=(S5_GROUPS,),
        in_specs=[pl.BlockSpec((1, C * B, w), lambda g: (g, 0, 0)),
                  pl.BlockSpec((1, w, w), lambda g: (g, 0, 0)),
                  pl.BlockSpec((2, 1, w, 256), lambda g: (0, g, 0, 0)),
                  pl.BlockSpec((2, 1, 128, w), lambda g: (0, g, 0, 0)),
                  pl.BlockSpec((2, 1, 3, 128), lambda g: (0, g, 0, 0))],
        out_specs=pl.BlockSpec((1, C * B, w), lambda g: (g, 0, 0)),
        scratch_shapes=[pltpu.VMEM((2, C * B, 256), F32), pltpu.VMEM((2, C * B, 128), F32)],
        compiler_params=_cparams("parallel"),
        name="s5_scan",
    )(ug, mt, fcat, e, a)
    return yg.reshape(S5_GROUPS, C, B, S5_CHUNK, S5_GROUP).transpose(2, 1, 3, 0, 4).reshape(B, S, S5_W)


def _s5_mats(a_re, a_im, log_dt, b_re, b_im, c_re, c_im):
    L = S5_CHUNK
    lam_re = jnp.minimum(a_re.astype(F32), -1e-4)
    lam_im = a_im.astype(F32)
    dt = jnp.exp(log_dt.astype(F32))[..., None]
    mag = jnp.exp(lam_re * dt)
    ab_re = mag * jnp.cos(lam_im * dt)
    ab_im = mag * jnp.sin(lam_im * dt)
    den = lam_re * lam_re + lam_im * lam_im
    n_re = ab_re - 1.0
    f_re = (n_re * lam_re + ab_im * lam_im) / den
    f_im = (ab_im * lam_re - n_re * lam_im) / den
    br = b_re.astype(F32)
    bi = b_im.astype(F32)
    bb_re = f_re[..., None] * br - f_im[..., None] * bi
    bb_im = f_re[..., None] * bi + f_im[..., None] * br
    tau = jnp.arange(L + 1, dtype=F32)
    pm = jnp.exp((lam_re * dt)[..., None] * tau)
    ph = (lam_im * dt)[..., None] * tau
    pw_re = pm * jnp.cos(ph)
    pw_im = pm * jnp.sin(ph)
    cr = c_re.astype(F32)
    ci = c_im.astype(F32)
    ca_re = cr[:, :, None] * pw_re.transpose(0, 1, 3, 2)[:, :, :, None, :] - ci[:, :, None] * pw_im.transpose(0, 1, 3, 2)[:, :, :, None, :]
    ca_im = cr[:, :, None] * pw_im.transpose(0, 1, 3, 2)[:, :, :, None, :] + ci[:, :, None] * pw_re.transpose(0, 1, 3, 2)[:, :, :, None, :]
    kk = (jnp.einsum('dgtnp,dgpm->dgtnm', ca_re, bb_re, precision=HIGHEST)
          - jnp.einsum('dgtnp,dgpm->dgtnm', ca_im, bb_im, precision=HIGHEST))
    t = np.arange(L)
    lag_f = t[:, None] - t[None, :]
    m_f = jnp.where((lag_f >= 0)[None, :, :, None, None], kk[0][:, np.clip(lag_f, 0, L)], 0.0)
    m_r = jnp.where((lag_f <= 0)[None, :, :, None, None], kk[1][:, np.clip(-lag_f, 0, L)], 0.0)
    mt = (m_f + m_r).transpose(0, 2, 4, 1, 3).reshape(S5_GROUPS, L * S5_GROUP, L * S5_GROUP)
    pow_f = np.stack([L - 1 - t, t])
    pr = jnp.stack([pw_re[d][:, :, pow_f[d]] for d in range(2)])
    pi = jnp.stack([pw_im[d][:, :, pow_f[d]] for d in range(2)])
    fr = pr[..., None] * bb_re[:, :, :, None, :] - pi[..., None] * bb_im[:, :, :, None, :]
    fi = pr[..., None] * bb_im[:, :, :, None, :] + pi[..., None] * bb_re[:, :, :, None, :]
    fr = fr.transpose(0, 1, 3, 4, 2).reshape(2, S5_GROUPS, L * S5_GROUP, S5_STATE)
    fi = fi.transpose(0, 1, 3, 4, 2).reshape(2, S5_GROUPS, L * S5_GROUP, S5_STATE)
    fcat = jnp.concatenate([fr, fi, fi, fr], axis=-1)
    pow_e = np.stack([t + 1, L - t])
    er = jnp.stack([ca_re[d][:, pow_e[d]] for d in range(2)])
    ei = jnp.stack([ca_im[d][:, pow_e[d]] for d in range(2)])
    er = er.transpose(0, 1, 4, 2, 3).reshape(2, S5_GROUPS, S5_STATE, L * S5_GROUP)
    ei = ei.transpose(0, 1, 4, 2, 3).reshape(2, S5_GROUPS, S5_STATE, L * S5_GROUP)
    e = jnp.concatenate([er, -ei], axis=2)
    ar = pw_re[..., L]
    ai = pw_im[..., L]
    a = jnp.stack([jnp.concatenate([ar, ar], -1), jnp.concatenate([-ai, ai], -1),
                   jnp.concatenate([ai, -ai], -1)], axis=2)
    return mt.astype(BF16), fcat.astype(BF16), e.astype(BF16), a


def _split_bf16(v):
    hi = v.astype(BF16)
    return hi, (v - hi.astype(F32)).astype(BF16)


def _merge_kernel(x_ref, mod_ref, oa_ref, ob_ref, ys_ref, u_ref, gate_ref, d_ref, wglu_ref, wbr_ref,
                  wout_ref, g2_ref, wrh_ref, wrl_ref, br_ref, xo_ref, h2_ref, lg_ref):
    yc = ys_ref[0].astype(F32) + u_ref[0].astype(F32) * d_ref[...]
    z = jnp.dot(jax.nn.gelu(yc).astype(BF16), wglu_ref[...], preferred_element_type=F32)
    oc = (z[:, :S5_W] * jax.nn.sigmoid(z[:, S5_W:])).astype(BF16)
    outs = (oa_ref[0], ob_ref[0], oc)
    y = None
    for i in range(N_BRANCH):
        t = gate_ref[0, :, i * D_MODEL:(i + 1) * D_MODEL].astype(F32) * jnp.dot(
            outs[i], wbr_ref[i], preferred_element_type=F32)
        y = t if y is None else y + t
    y2 = jnp.dot(y.astype(BF16), wout_ref[...], preferred_element_type=F32)
    gt1 = mod_ref[0, 0, 2:3, :]
    sh2 = mod_ref[0, 0, 3:4, :]
    sc2 = mod_ref[0, 0, 4:5, :]
    xn = x_ref[0] + gt1 * y2
    xo_ref[0] = xn
    h2 = _rms(xn, g2_ref[...]) * (1.0 + sc2) + sh2
    hi, lo = _split_bf16(h2)
    h2_ref[0] = hi
    lg_ref[0] = (jnp.dot(hi, wrh_ref[...], preferred_element_type=F32)
                 + jnp.dot(lo, wrh_ref[...], preferred_element_type=F32)
                 + jnp.dot(hi, wrl_ref[...], preferred_element_type=F32) + br_ref[...])


def _merge(xs, mods, oa, ob, ys, u, gates, d_skip, wglu, wbr, wout, g2, wrh, wrl, br, tile_off):
    B, S, D = xs.shape
    nt = S // TM - tile_off
    so = nt * TM
    tok = lambda w: pl.BlockSpec((1, TM, w), lambda b, j: (b, j + tile_off, 0))
    otok = lambda w: pl.BlockSpec((1, TM, w), lambda b, j: (b, j, 0))
    return pl.pallas_call(
        _merge_kernel,
        out_shape=[jax.ShapeDtypeStruct((B, so, D), F32), jax.ShapeDtypeStruct((B, so, D), BF16),
                   jax.ShapeDtypeStruct((B, so, 128), F32)],
        grid=(B, nt),
        in_specs=[tok(D),
                  pl.BlockSpec((1, 1, 6, D), lambda b, j: (b, jnp.minimum(j + tile_off, 1), 0, 0)),
                  tok(BRANCH_W), tok(BRANCH_W), tok(S5_W), tok(S5_W), tok(N_BRANCH * D_MODEL),
                  _const_spec((1, S5_W)), _const_spec(wglu.shape), _const_spec(wbr.shape),
                  _const_spec(wout.shape), _const_spec((1, D)), _const_spec(wrh.shape),
                  _const_spec(wrl.shape), _const_spec((1, 128))],
        out_specs=[otok(D), otok(D), otok(128)],
        compiler_params=_cparams("parallel", "arbitrary"),
        name="merge_router",
    )(xs, mods, oa, ob, ys, u, gates, d_skip, wglu, wbr, wout, g2, wrh, wrl, br)


def _expert_kernel(be_ref, nu_ref, x_ref, wup_ref, wdn_ref, y_ref):
    i = pl.program_id(0)

    @pl.when(i < nu_ref[0])
    def _():
        a = jnp.dot(x_ref[...], wup_ref[0], preferred_element_type=F32)
        g = a[:, :EXPERT_HIDDEN]
        hid = (g * jax.nn.sigmoid(g) * a[:, EXPERT_HIDDEN:]).astype(BF16)
        y_ref[...] = jnp.dot(hid, wdn_ref[0], preferred_element_type=F32).astype(BF16)

    @pl.when(i >= nu_ref[0])
    def _():
        y_ref[...] = jnp.zeros(y_ref.shape, BF16)


def _experts(xg, blk_e, n_used, w_up, w_down):
    n_rows, D = xg.shape
    n_blk = n_rows // MOE_BLOCK
    gs = pltpu.PrefetchScalarGridSpec(
        num_scalar_prefetch=2,
        grid=(n_blk,),
        in_specs=[pl.BlockSpec((MOE_BLOCK, D), lambda i, be, nu: (i, 0)),
                  pl.BlockSpec((1, D, 2 * EXPERT_HIDDEN), lambda i, be, nu: (be[i], 0, 0)),
                  pl.BlockSpec((1, EXPERT_HIDDEN, D), lambda i, be, nu: (be[i], 0, 0))],
        out_specs=pl.BlockSpec((MOE_BLOCK, D), lambda i, be, nu: (i, 0)),
    )
    return pl.pallas_call(
        _expert_kernel,
        out_shape=jax.ShapeDtypeStruct((n_rows, D), BF16),
        grid_spec=gs,
        compiler_params=_cparams("arbitrary"),
        name="moe_experts",
    )(blk_e, n_used, xg, w_up, w_down)


def _route(logits, b_unused=None):
    gl = logits[:, :N_GROUPS]
    grp = jnp.argmax(gl, axis=-1)
    p_grp = 1.0 / jnp.sum(jnp.exp(gl - jnp.max(gl, axis=-1, keepdims=True)), axis=-1)
    el = logits[:, N_GROUPS:N_GROUPS + N_EXPERTS].reshape(-1, N_GROUPS, EXPERTS_PER_GROUP)
    el = jnp.take_along_axis(el, grp[:, None, None], axis=1)[:, 0]
    top_v, top_i = lax.top_k(el, TOP_K)
    gate = p_grp[:, None] * jax.nn.softmax(top_v, axis=-1)
    eid = grp[:, None].astype(jnp.int32) * EXPERTS_PER_GROUP + top_i.astype(jnp.int32)
    return eid, gate


def _moe(h2, logits, w_up, w_down):
    T, D = h2.shape
    eid, gate = _route(logits)
    S2 = T * TOP_K
    e_flat = eid.reshape(S2)
    onehot = (e_flat[:, None] == jnp.arange(N_EXPERTS, dtype=jnp.int32)[None, :]).astype(jnp.int32)
    csum = jnp.cumsum(onehot, axis=0)
    rank = jnp.take_along_axis(csum, e_flat[:, None], axis=1)[:, 0] - 1
    counts = csum[-1]
    padded = (counts + MOE_BLOCK - 1) // MOE_BLOCK * MOE_BLOCK
    p_end = jnp.cumsum(padded)
    p_start = p_end - padded
    dest = p_start[e_flat] + rank
    n_blk = -(-(S2 + N_EXPERTS * (MOE_BLOCK - 1)) // MOE_BLOCK)
    tok_flat = jnp.repeat(jnp.arange(T, dtype=jnp.int32), TOP_K)
    buf_tok = jnp.zeros((n_blk * MOE_BLOCK,), jnp.int32).at[dest].set(tok_flat)
    blk_e = jnp.minimum(jnp.searchsorted(p_end, jnp.arange(n_blk, dtype=jnp.int32) * MOE_BLOCK, side='right'),
                        N_EXPERTS - 1).astype(jnp.int32)
    n_used = (p_end[-1] // MOE_BLOCK).astype(jnp.int32).reshape(1)
    xg = jnp.take(h2, buf_tok, axis=0)
    y = _experts(xg, blk_e, n_used, w_up, w_down)
    pos = dest.reshape(T, TOP_K)
    return (gate[:, 0:1] * jnp.take(y, pos[:, 0], axis=0).astype(F32)
            + gate[:, 1:2] * jnp.take(y, pos[:, 1], axis=0).astype(F32))


def _rope_swap_index():
    i = np.arange(MLA_ROPE)
    return (i // 16) * 16 + (1 - (i % 16) // 8) * 8 + i % 8


def _pack_w_in(w_in):
    sw = _rope_swap_index()
    kr = w_in[:, C_KR:C_NK]
    z64 = jnp.zeros((w_in.shape[0], 64), w_in.dtype)
    return jnp.concatenate([w_in[:, C_KV:C_KR], z64, kr, kr[:, sw], w_in[:, C_NK:]], axis=1).astype(BF16)


def _pack_w_uq(w_uq):
    sw = _rope_swap_index()
    w = w_uq.reshape(Q_LORA, HEADS, MLA_QK)
    return jnp.concatenate([w, w[:, :, MLA_NOPE:][:, :, sw]], axis=-1).reshape(Q_LORA, HEADS * HP).astype(BF16)


def _pack_w_ukv(w_ukv):
    w = w_ukv.reshape(KV_LORA, HEADS, MLA_NOPE + MLA_V)
    z = jnp.zeros((KV_LORA, HEADS, 64), w.dtype)
    kn = jnp.concatenate([w[:, :, :MLA_NOPE], z], axis=-1)
    v = w[:, :, MLA_NOPE:]
    even = (np.arange(HEADS) % 2 == 0)[None, :, None]
    vp = jnp.concatenate([jnp.where(even, v, 0.0), jnp.where(even, 0.0, v)], axis=-1)
    return jnp.concatenate([kn.reshape(KV_LORA, HEADS * HP), vp.reshape(KV_LORA, HEADS * HP)], axis=1).astype(BF16)


def _rope_lane_tables(L):
    t = np.arange(L)
    nf = MLA_ROPE // 4
    inv = ROPE_THETA ** (-np.arange(nf, dtype=np.float64) / nf)
    ang = np.concatenate([(t // GRID_W)[:, None] * inv, (t % GRID_W)[:, None] * inv], axis=-1)
    i = np.arange(MLA_ROPE)
    col = (i // 16) * nf + i % 8
    sgn = np.where((i % 16) // 8 == 0, -1.0, 1.0)
    cos = np.concatenate([np.ones((TM, MLA_ROPE)), np.cos(ang)[:, col]], axis=0)
    sin = np.concatenate([np.zeros((TM, MLA_ROPE)), np.sin(ang)[:, col] * sgn], axis=0)
    return cos.astype(np.float32), sin.astype(np.float32)


def _head_lane_weights(g, cos, sin, scale):
    sw = _rope_swap_index()
    S = cos.shape[0]
    g = g.astype(F32) * scale
    w1 = jnp.concatenate([jnp.broadcast_to(g[:MLA_NOPE], (S, MLA_NOPE)), g[MLA_NOPE:] * cos,
                          jnp.zeros((S, 32), F32)], axis=1)
    w2 = jnp.concatenate([jnp.zeros((S, MLA_NOPE), F32), g[MLA_NOPE:][sw] * sin, jnp.zeros((S, 32), F32)], axis=1)
    return w1, w2


def kernel(x, c, ctx, c_ctx, w_ada, b_ada, g_norm1, w_in, g_q_lat, w_uq, g_kv_lat, w_ukv, g_qn_mla, g_kn_mla,
           g_qn_nat, g_kn_nat, nat_rpb, s5_a_re, s5_a_im, s5_log_dt, s5_b_re, s5_b_im, s5_c_re, s5_c_im, s5_d,
           w_glu, w_branch, w_out, g_norm2, w_group, b_group, w_expert, b_expert, w_up, w_down):
    B, L, D = x.shape
    Cx = ctx.shape[1]
    assert Cx == TM and D == D_MODEL and L % GRID_W == 0 and L // GRID_W >= WIN_H
    depth = w_in.shape[0]
    S = Cx + L
    cos, sin = _rope_lane_tables(L)
    rows_pad = -(-(B + 1) // 8) * 8
    c_all = jnp.concatenate([c, c_ctx[None], jnp.zeros((rows_pad - B - 1, D), c.dtype)], axis=0).astype(F32)

    xs = jnp.concatenate([ctx, x], axis=1).astype(F32)
    for l in range(depth):
        need_ctx = l < depth - 1
        ada = _ada(c_all, w_ada[l], b_ada[l])
        mods = jnp.stack([jnp.broadcast_to(ada[B].reshape(1, 6, D), (B, 6, D)), ada[:B].reshape(B, 6, D)], axis=1)

        kw1, kw2 = _head_lane_weights(g_kn_mla[l], cos, sin, 1.0)
        qw1, qw2 = _head_lane_weights(g_qn_mla[l], cos, sin, MLA_QK ** -0.5)
        gnq = jnp.tile(g_qn_nat[l].astype(F32) * NAT_DIM ** -0.5, 2).reshape(1, 128)
        gnk = jnp.tile(g_kn_nat[l].astype(F32), 2).reshape(1, 128)
        qm, km, vm, qn, kn, vn, u, gates = _front(
            xs, mods, g_norm1[l].reshape(1, D).astype(F32), _pack_w_in(w_in[l]),
            g_kv_lat[l].reshape(1, KV_LORA).astype(F32), _pack_w_ukv(w_ukv[l]),
            g_q_lat[l].reshape(1, Q_LORA).astype(F32), _pack_w_uq(w_uq[l]), kw1, kw2, qw1, qw2, gnq, gnk)

        oa = _mla(qm, km, vm, need_ctx)
        ob = _nat(qn, kn, vn, _nat_bias_table(nat_rpb[l]), need_ctx)
        ys = _s5(u, _s5_mats(s5_a_re[l], s5_a_im[l], s5_log_dt[l], s5_b_re[l], s5_b_im[l], s5_c_re[l], s5_c_im[l]))

        w_route = jnp.concatenate([w_group[l], w_expert[l], jnp.zeros((D, 128 - N_GROUPS - N_EXPERTS), F32)],
                                  axis=1).astype(F32)
        wrh, wrl = _split_bf16(w_route)
        b_route = jnp.concatenate([b_group[l], b_expert[l], jnp.zeros((128 - N_GROUPS - N_EXPERTS,), F32)]).reshape(1, 128)
        tile_off = 0 if need_ctx else 1
        x_mid, h2, logits = _merge(
            xs, mods, oa, ob, ys, u, gates, s5_d[l].reshape(1, S5_W).astype(F32), w_glu[l].astype(BF16),
            w_branch[l].astype(BF16), w_out[l].astype(BF16), g_norm2[l].reshape(1, D).astype(F32), wrh, wrl,
            b_route.astype(F32), tile_off)

        So = x_mid.shape[1]
        f = _moe(h2.reshape(B * So, D), logits.reshape(B * So, 128), w_up[l].astype(BF16), w_down[l].astype(BF16))
        gt2 = mods[:, 1, 5][:, None, :]
        if need_ctx:
            gt2_all = jnp.concatenate([jnp.broadcast_to(mods[:, 0, 5][:, None, :], (B, Cx, D)),
                                       jnp.broadcast_to(gt2, (B, L, D))], axis=1)
            xs = x_mid + gt2_all * f.reshape(B, So, D)
        else:
            xs = x_mid + gt2 * f.reshape(B, So, D)
    return xs.astype(x.dtype)
```

```python
import functools

import numpy as np
import jax
import jax.numpy as jnp
from jax import lax
from jax.experimental import pallas as pl
from jax.experimental.pallas import tpu as pltpu

F32 = jnp.float32
BF16 = jnp.bfloat16
HIGHEST = lax.Precision.HIGHEST

D_MODEL = 1024
GRID_W = 64
N_BRANCH = 3
BRANCH_W = 512
RMS_EPS = 1e-6
NEG = -1e30
HEADS = 8
MLA_NOPE = 64
MLA_ROPE = 32
MLA_QK = 96
MLA_V = 64
KV_LORA = 256
Q_LORA = 768
ROPE_THETA = 10000.0
NAT_DIM = 64
WIN_H = 8
WIN_W = 16
S5_W = 512
S5_GROUP = 16
S5_GROUPS = 32
S5_STATE = 64
N_GROUPS = 4
EXPERTS_PER_GROUP = 8
N_EXPERTS = 32
EXPERT_HIDDEN = 512
TOP_K = 2

CTX = 256
TM = 768
S5_CHUNK = 16
MOE_BLOCK = 512
HP = 128
NAT_QROWS = 4
NAT_KROWS = NAT_QROWS + WIN_H
VMEM_LIMIT = 56 * 1024 * 1024

P_KV = 0
P_KR = P_KV + KV_LORA
P_NK = P_KR + HP
P_NV = P_NK + 512
P_S5 = P_NV + 512
P_QL = P_S5 + 512
P_NQ = P_QL + Q_LORA
P_GATE = P_NQ + 512
P_END = P_GATE + N_BRANCH * D_MODEL

C_KV = 0
C_KR = C_KV + KV_LORA
C_NK = C_KR + MLA_ROPE
C_NV = C_NK + 512
C_S5 = C_NV + 512
C_QL = C_S5 + S5_W
C_NQ = C_QL + Q_LORA
C_GATE = C_NQ + 512


def _cparams(*sem):
    return pltpu.CompilerParams(dimension_semantics=sem, vmem_limit_bytes=VMEM_LIMIT)


def _const_spec(shape):
    nd = len(shape)
    return pl.BlockSpec(shape, lambda *_: (0,) * nd, pipeline_mode=pl.Buffered(1))


def _ada_kernel(c_ref, w_ref, b_ref, o_ref):
    c = c_ref[...]
    s = (c * jax.nn.sigmoid(c)).astype(BF16)
    o_ref[...] = jnp.dot(s, w_ref[...].astype(BF16), preferred_element_type=F32) + b_ref[...]


def _ada(c_all, w_ada, b_ada):
    rows, d = c_all.shape
    n = w_ada.shape[1]
    bn = 512
    return pl.pallas_call(
        _ada_kernel,
        out_shape=jax.ShapeDtypeStruct((rows, n), F32),
        grid=(n // bn,),
        in_specs=[pl.BlockSpec((rows, d), lambda i: (0, 0)),
                  pl.BlockSpec((d, bn), lambda i: (0, i)),
                  pl.BlockSpec((1, bn), lambda i: (0, i))],
        out_specs=pl.BlockSpec((rows, bn), lambda i: (0, i)),
        compiler_params=_cparams("arbitrary"),
        name="ada_mod",
    )(c_all, w_ada, b_ada.reshape(1, n))


def _tile_mod(mod_ref, idx, tile):
    ctx_v = mod_ref[0, 0, idx:idx + 1, :]
    lat_v = mod_ref[0, 1, idx:idx + 1, :]
    row = lax.broadcasted_iota(jnp.int32, (TM, 1), 0) + tile * TM
    return jnp.where(row < CTX, ctx_v, lat_v)


def _rms(x, g):
    return x * lax.rsqrt(jnp.mean(x * x, axis=-1, keepdims=True) + RMS_EPS) * g


def _mla_head(xh, w1, w2):
    lane = lax.broadcasted_iota(jnp.int32, xh.shape, 1)
    sq = jnp.where(lane < MLA_QK, xh * xh, 0.0)
    inv = lax.rsqrt(jnp.sum(sq, axis=-1, keepdims=True) * (1.0 / MLA_QK) + RMS_EPS)
    return inv * (xh * w1 + pltpu.roll(xh, HP - MLA_ROPE, 1) * w2)


def _pair_norm(x2, g2):
    lane = lax.broadcasted_iota(jnp.int32, x2.shape, 1)
    lo = lane < NAT_DIM
    sq = x2 * x2
    s_lo = jnp.sum(jnp.where(lo, sq, 0.0), axis=-1, keepdims=True)
    s_hi = jnp.sum(jnp.where(lo, 0.0, sq), axis=-1, keepdims=True)
    inv = lax.rsqrt(jnp.where(lo, s_lo, s_hi) * (1.0 / NAT_DIM) + RMS_EPS)
    return x2 * inv * g2


def _front_kernel(x_ref, mod_ref, g1_ref, win_ref, gkv_ref, wukv_ref, gq_ref, wuq_ref,
                  kw1_ref, kw2_ref, qw1_ref, qw2_ref, gnq_ref, gnk_ref,
                  qm_ref, km_ref, vm_ref, qn_ref, kn_ref, vn_ref, u_ref, gate_ref):
    j = pl.program_id(1)
    sh = _tile_mod(mod_ref, 0, j)
    sc = _tile_mod(mod_ref, 1, j)
    h = (_rms(x_ref[0], g1_ref[...]) * (1.0 + sc) + sh).astype(BF16)

    def proj(a, b):
        return jnp.dot(h, win_ref[:, a:b], preferred_element_type=F32)

    kvl = _rms(proj(P_KV, P_KR), gkv_ref[...]).astype(BF16)
    krp = proj(P_KR, P_NK)
    kw1 = kw1_ref[...]
    kw2 = kw2_ref[...]
    for hd in range(HEADS):
        kn = jnp.dot(kvl, wukv_ref[:, hd * HP:(hd + 1) * HP], preferred_element_type=F32)
        km_ref[0, :, hd * HP:(hd + 1) * HP] = _mla_head(kn + krp, kw1, kw2).astype(BF16)
    vm_ref[0] = jnp.dot(kvl, wukv_ref[:, HEADS * HP:], preferred_element_type=F32).astype(BF16)

    ql = _rms(proj(P_QL, P_NQ), gq_ref[...]).astype(BF16)
    qw1 = qw1_ref[...]
    qw2 = qw2_ref[...]
    for hd in range(HEADS):
        qh = jnp.dot(ql, wuq_ref[:, hd * HP:(hd + 1) * HP], preferred_element_type=F32)
        qm_ref[0, :, hd * HP:(hd + 1) * HP] = _mla_head(qh, qw1, qw2).astype(BF16)

    gnq = gnq_ref[...]
    gnk = gnk_ref[...]
    for c in range(4):
        kk = proj(P_NK + c * 128, P_NK + (c + 1) * 128)
        kn_ref[0, :, c * 128:(c + 1) * 128] = _pair_norm(kk, gnk).astype(BF16)
        qq = proj(P_NQ + c * 128, P_NQ + (c + 1) * 128)
        qn_ref[0, :, c * 128:(c + 1) * 128] = _pair_norm(qq, gnq).astype(BF16)
    vn_ref[0] = proj(P_NV, P_S5).astype(BF16)

    u_ref[0] = proj(P_S5, P_QL).astype(BF16)
    for c in range(N_BRANCH * D_MODEL // 512):
        gate_ref[0, :, c * 512:(c + 1) * 512] = jax.nn.sigmoid(
            proj(P_GATE + c * 512, P_GATE + (c + 1) * 512)).astype(BF16)


def _front(xs, mods, g1, w_in_p, gkv, wukv_p, gq, wuq_p, kw1, kw2, qw1, qw2, gnq, gnk):
    B, S, D = xs.shape
    nt = S // TM
    tok = lambda w: pl.BlockSpec((1, TM, w), lambda b, j: (b, j, 0))
    tab = pl.BlockSpec((TM, HP), lambda b, j: (j, 0))
    outs = [(HEADS * HP, BF16), (HEADS * HP, BF16), (HEADS * HP, BF16), (512, BF16), (512, BF16),
            (512, BF16), (S5_W, BF16), (N_BRANCH * D_MODEL, BF16)]
    return pl.pallas_call(
        _front_kernel,
        out_shape=[jax.ShapeDtypeStruct((B, S, w), dt) for w, dt in outs],
        grid=(B, nt),
        in_specs=[tok(D),
                  pl.BlockSpec((1, 2, 6, D), lambda b, j: (b, 0, 0, 0)),
                  _const_spec((1, D)), _const_spec(w_in_p.shape), _const_spec((1, KV_LORA)),
                  _const_spec(wukv_p.shape), _const_spec((1, Q_LORA)), _const_spec(wuq_p.shape),
                  tab, tab, tab, tab, _const_spec((1, 128)), _const_spec((1, 128))],
        out_specs=[tok(w) for w, _ in outs],
        compiler_params=_cparams("parallel", "arbitrary"),
        name="mixer_front",
    )(xs, mods, g1, w_in_p, gkv, wukv_p, gq, wuq_p, kw1, kw2, qw1, qw2, gnq, gnk)


def _nt(a, b):
    return lax.dot_general(a, b, (((1,), (1,)), ((), ())), preferred_element_type=F32)


def _softmax_pv(s, v):
    m = jnp.max(s, axis=-1, keepdims=True)
    p = jnp.exp(s - m)
    l = jnp.sum(p, axis=-1, keepdims=True)
    return jnp.dot(p.astype(BF16), v, preferred_element_type=F32) * (1.0 / l)


def _mla_kernel(q_ref, k_ref, v_ref, o_ref, *, seq, need_ctx):
    j = pl.program_id(2)

    def attend(r0, nr, nk):
        o = None
        for hh in range(2):
            sl = slice(hh * HP, (hh + 1) * HP)
            s = _nt(q_ref[0, r0:r0 + nr, sl], k_ref[0, :nk, sl])
            t = _softmax_pv(s, v_ref[0, :nk, sl])
            o = t if o is None else o + t
        o_ref[0, r0:r0 + nr, :] = o.astype(BF16)

    @pl.when(j == 0)
    def _():
        if need_ctx:
            attend(0, CTX, CTX)
        else:
            o_ref[0, 0:CTX, :] = jnp.zeros((CTX, 128), BF16)
        attend(CTX, TM - CTX, seq)

    @pl.when(j > 0)
    def _():
        attend(0, TM, seq)


def _mla(qm, km, vm, need_ctx):
    B, S, _ = qm.shape
    nt = S // TM
    return pl.pallas_call(
        functools.partial(_mla_kernel, seq=S, need_ctx=need_ctx),
        out_shape=jax.ShapeDtypeStruct((B, S, BRANCH_W), BF16),
        grid=(B, HEADS // 2, nt),
        in_specs=[pl.BlockSpec((1, TM, 2 * HP), lambda b, h, j: (b, j, h)),
                  pl.BlockSpec((1, S, 2 * HP), lambda b, h, j: (b, 0, h)),
                  pl.BlockSpec((1, S, 2 * HP), lambda b, h, j: (b, 0, h))],
        out_specs=pl.BlockSpec((1, TM, 128), lambda b, h, j: (b, j, h)),
        compiler_params=_cparams("parallel", "parallel", "arbitrary"),
        name="mla_attn",
    )(qm, km, vm)


def _nat_kernel(q_ref, k_ref, v_ref, bias_ref, o_ref, *, rows, need_ctx):
    lane = lax.broadcasted_iota(jnp.int32, (1, 128), 1)
    lo = lane < NAT_DIM
    masks = (lo, jnp.logical_not(lo))
    nq = NAT_QROWS * GRID_W
    nk = NAT_KROWS * GRID_W
    n_blocks = rows // NAT_QROWS

    def half(x, hh):
        return jnp.where(masks[hh], x, jnp.zeros_like(x))

    kc = k_ref[0, 0:CTX, :]
    vc = v_ref[0, 0:CTX, :]

    if need_ctx:
        qc = q_ref[0, 0:CTX, :]
        o = None
        for hh in range(2):
            t = _softmax_pv(_nt(half(qc, hh), kc), half(vc, hh))
            o = t if o is None else o + t
        o_ref[0, 0:CTX, :] = o.astype(BF16)
    else:
        o_ref[0, 0:CTX, :] = jnp.zeros((CTX, 128), BF16)

    def block(i, carry):
        ws = jnp.clip(i * NAT_QROWS - WIN_H // 2, 0, rows - NAT_KROWS)
        cls = jnp.where(i == 0, 0, jnp.where(i == n_blocks - 1, 2, 1))
        q0 = pl.multiple_of(CTX + i * nq, nq)
        k0 = pl.multiple_of(CTX + ws * GRID_W, GRID_W)
        q = q_ref[0, pl.ds(q0, nq), :]
        kw = k_ref[0, pl.ds(k0, nk), :]
        vw = v_ref[0, pl.ds(k0, nk), :]
        o = None
        for hh in range(2):
            qh = half(q, hh)
            s_lat = _nt(qh, kw) + bias_ref[0, hh, cls]
            s_ctx = _nt(qh, kc)
            m = jnp.maximum(jnp.max(s_lat, axis=-1, keepdims=True), jnp.max(s_ctx, axis=-1, keepdims=True))
            p_lat = jnp.exp(s_lat - m)
            p_ctx = jnp.exp(s_ctx - m)
            l = jnp.sum(p_lat, axis=-1, keepdims=True) + jnp.sum(p_ctx, axis=-1, keepdims=True)
            acc = (jnp.dot(p_lat.astype(BF16), half(vw, hh), preferred_element_type=F32)
                   + jnp.dot(p_ctx.astype(BF16), half(vc, hh), preferred_element_type=F32))
            t = acc * (1.0 / l)
            o = t if o is None else o + t
        o_ref[0, pl.ds(q0, nq), :] = o.astype(BF16)
        return carry

    lax.fori_loop(0, n_blocks, block, 0)


def _nat(qn, kn, vn, bias, need_ctx):
    B, S, _ = qn.shape
    rows = (S - CTX) // GRID_W
    seq = pl.BlockSpec((1, S, 128), lambda h, b: (b, 0, h))
    return pl.pallas_call(
        functools.partial(_nat_kernel, rows=rows, need_ctx=need_ctx),
        out_shape=jax.ShapeDtypeStruct((B, S, BRANCH_W), BF16),
        grid=(HEADS // 2, B),
        in_specs=[seq, seq, seq,
                  pl.BlockSpec((1, 2, 3, NAT_QROWS * GRID_W, NAT_KROWS * GRID_W), lambda h, b: (h, 0, 0, 0, 0))],
        out_specs=seq,
        compiler_params=_cparams("parallel", "arbitrary"),
        name="nat_attn",
    )(qn, kn, vn, bias)


def _nat_bias_table(rpb, rows):
    assert rows % NAT_QROWS == 0 and rows >= NAT_KROWS
    n_blocks = rows // NAT_QROWS
    cols = np.arange(GRID_W)
    cs = np.clip(cols - WIN_W // 2, 0, GRID_W - WIN_W)
    kcol = cols[None, :]
    qcol = cols[:, None]
    vcol = (kcol >= cs[:, None]) & (kcol < cs[:, None] + WIN_W)
    dcol = np.clip(kcol - qcol + WIN_W - 1, 0, 2 * WIN_W - 2)
    blk = np.array([0, 1, n_blocks - 1])
    ws = np.clip(blk * NAT_QROWS - WIN_H // 2, 0, rows - NAT_KROWS)
    r = blk[:, None] * NAT_QROWS + np.arange(NAT_QROWS)[None, :]
    rs = np.clip(r - WIN_H // 2, 0, rows - WIN_H)
    krow = ws[:, None, None] + np.arange(NAT_KROWS)[None, None, :]
    vrow = (krow >= rs[:, :, None]) & (krow < rs[:, :, None] + WIN_H)
    drow = np.clip(krow - r[:, :, None] + WIN_H - 1, 0, 2 * WIN_H - 2)
    t = rpb.astype(F32)[:, drow]
    t = t[..., dcol]
    valid = vrow[:, :, :, None, None] & vcol[None, None, None]
    t = jnp.where(valid[None], t, NEG)
    t = t.transpose(0, 1, 2, 4, 3, 5).reshape(HEADS, 3, NAT_QROWS * GRID_W, NAT_KROWS * GRID_W)
    return t.reshape(HEADS // 2, 2, 3, NAT_QROWS * GRID_W, NAT_KROWS * GRID_W)


def _s5_kernel(u_ref, mt_ref, f_ref, e_ref, a_ref, y_ref, s_scr, hin_scr, *, batch, n_ctx, n_all):
    u = u_ref[0]
    for d in range(2):
        s_scr[d] = jnp.dot(u, f_ref[d, 0], preferred_element_type=F32)

    def make_step(d):
        a1 = a_ref[d, 0, 0:1, :]
        a2 = a_ref[d, 0, 1:2, :]
        a3 = a_ref[d, 0, 2:3, :]

        def step(c, hh):
            h, hs = hh
            r0 = pl.multiple_of(c * batch, batch)
            hin_scr[d, pl.ds(r0, batch), :] = h
            s = s_scr[d, pl.ds(r0, batch), :]
            return (h * a1 + hs * a2 + s[:, :128], hs * a1 + h * a3 + s[:, 128:])
        return step

    z = jnp.zeros((batch, 128), F32)
    lax.fori_loop(0, n_all, make_step(0), (z, z))
    rstep = make_step(1)
    hh = lax.fori_loop(0, n_ctx, lambda i, c: rstep(n_ctx - 1 - i, c), (z, z))
    lax.fori_loop(0, n_all - n_ctx, lambda i, c: rstep(n_all - 1 - i, c), hh)

    y = jnp.dot(u, mt_ref[0], preferred_element_type=F32)
    for d in range(2):
        y = y + jnp.dot(hin_scr[d].astype(BF16), e_ref[d, 0], preferred_element_type=F32)
    y_ref[0] = y.astype(BF16)


def _s5(u, mats):
    mt, fcat, e, a = mats
    B, S, _ = u.shape
    C = S // S5_CHUNK
    ug = u.reshape(B, C, S5_CHUNK, S5_GROUPS, S5_GROUP).transpose(3, 1, 0, 2, 4).reshape(
        S5_GROUPS, C * B, S5_CHUNK * S5_GROUP)
    w = S5_CHUNK * S5_GROUP
    yg = pl.pallas_call(
        functools.partial(_s5_kernel, batch=B, n_ctx=CTX // S5_CHUNK, n_all=C),
        out_shape=jax.ShapeDtypeStruct((S5_GROUPS, C * B, w), BF16),
        grid=(S5_GROUPS,),
        in_specs=[pl.BlockSpec((1, C * B, w), lambda g: (g, 0, 0)),
                  pl.BlockSpec((1, w, w), lambda g: (g, 0, 0)),
                  pl.BlockSpec((2, 1, w, 256), lambda g: (0, g, 0, 0)),
                  pl.BlockSpec((2, 1, 128, w), lambda g: (0, g, 0, 0)),
                  pl.BlockSpec((2, 1, 3, 128), lambda g: (0, g, 0, 0))],
        out_specs=pl.BlockSpec((1, C * B, w), lambda g: (g, 0, 0)),
        scratch_shapes=[pltpu.VMEM((2, C * B, 256), F32), pltpu.VMEM((2, C * B, 128), F32)],
        compiler_params=_cparams("parallel"),
        name="s5_scan",
    )(ug, mt, fcat, e, a)
    return yg.reshape(S5_GROUPS, C, B, S5_CHUNK, S5_GROUP).transpose(2, 1, 3, 0, 4).reshape(B, S, S5_W)


def _s5_mats(a_re, a_im, log_dt, b_re, b_im, c_re, c_im):
    L = S5_CHUNK
    lam_re = jnp.minimum(a_re.astype(F32), -1e-4)
    lam_im = a_im.astype(F32)
    dt = jnp.exp(log_dt.astype(F32))[..., None]
    mag = jnp.exp(lam_re * dt)
    ab_re = mag * jnp.cos(lam_im * dt)
    ab_im = mag * jnp.sin(lam_im * dt)
    den = lam_re * lam_re + lam_im * lam_im
    n_re = ab_re - 1.0
    f_re = (n_re * lam_re + ab_im * lam_im) / den
    f_im = (ab_im * lam_re - n_re * lam_im) / den
    br = b_re.astype(F32)
    bi = b_im.astype(F32)
    bb_re = f_re[..., None] * br - f_im[..., None] * bi
    bb_im = f_re[..., None] * bi + f_im[..., None] * br
    tau = jnp.arange(L + 1, dtype=F32)
    pm = jnp.exp((lam_re * dt)[..., None] * tau)
    ph = (lam_im * dt)[..., None] * tau
    pw_re = pm * jnp.cos(ph)
    pw_im = pm * jnp.sin(ph)
    cr = c_re.astype(F32)
    ci = c_im.astype(F32)
    pwt_re = pw_re.transpose(0, 1, 3, 2)[:, :, :, None, :]
    pwt_im = pw_im.transpose(0, 1, 3, 2)[:, :, :, None, :]
    ca_re = cr[:, :, None] * pwt_re - ci[:, :, None] * pwt_im
    ca_im = cr[:, :, None] * pwt_im + ci[:, :, None] * pwt_re
    kk = (jnp.einsum('dgtnp,dgpm->dgtnm', ca_re, bb_re, precision=HIGHEST)
          - jnp.einsum('dgtnp,dgpm->dgtnm', ca_im, bb_im, precision=HIGHEST))
    t = np.arange(L)
    lag_f = t[:, None] - t[None, :]
    m_f = jnp.where((lag_f >= 0)[None, :, :, None, None], kk[0][:, np.clip(lag_f, 0, L)], 0.0)
    m_r = jnp.where((lag_f <= 0)[None, :, :, None, None], kk[1][:, np.clip(-lag_f, 0, L)], 0.0)
    mt = (m_f + m_r).transpose(0, 2, 4, 1, 3).reshape(S5_GROUPS, L * S5_GROUP, L * S5_GROUP)
    pow_f = np.stack([L - 1 - t, t])
    pr = jnp.stack([pw_re[d][:, :, pow_f[d]] for d in range(2)])
    pi = jnp.stack([pw_im[d][:, :, pow_f[d]] for d in range(2)])
    fr = pr[..., None] * bb_re[:, :, :, None, :] - pi[..., None] * bb_im[:, :, :, None, :]
    fi = pr[..., None] * bb_im[:, :, :, None, :] + pi[..., None] * bb_re[:, :, :, None, :]
    fr = fr.transpose(0, 1, 3, 4, 2).reshape(2, S5_GROUPS, L * S5_GROUP, S5_STATE)
    fi = fi.transpose(0, 1, 3, 4, 2).reshape(2, S5_GROUPS, L * S5_GROUP, S5_STATE)
    fcat = jnp.concatenate([fr, fi, fi, fr], axis=-1)
    pow_e = np.stack([t + 1, L - t])
    er = jnp.stack([ca_re[d][:, pow_e[d]] for d in range(2)])
    ei = jnp.stack([ca_im[d][:, pow_e[d]] for d in range(2)])
    er = er.transpose(0, 1, 4, 2, 3).reshape(2, S5_GROUPS, S5_STATE, L * S5_GROUP)
    ei = ei.transpose(0, 1, 4, 2, 3).reshape(2, S5_GROUPS, S5_STATE, L * S5_GROUP)
    e = jnp.concatenate([er, -ei], axis=2)
    ar = pw_re[..., L]
    ai = pw_im[..., L]
    a = jnp.stack([jnp.concatenate([ar, ar], -1), jnp.concatenate([-ai, ai], -1),
                   jnp.concatenate([ai, -ai], -1)], axis=2)
    return mt.astype(BF16), fcat.astype(BF16), e.astype(BF16), a


def _split_bf16(v):
    hi = v.astype(BF16)
    return hi, (v - hi.astype(F32)).astype(BF16)


def _merge_kernel(x_ref, mod_ref, oa_ref, ob_ref, ys_ref, u_ref, gate_ref, d_ref, wglu_ref, wbr_ref,
                  wout_ref, g2_ref, wrh_ref, wrl_ref, br_ref, xo_ref, h2_ref, lg_ref):
    j = pl.program_id(1)
    yc = ys_ref[0].astype(F32) + u_ref[0].astype(F32) * d_ref[...]
    z = jnp.dot(jax.nn.gelu(yc).astype(BF16), wglu_ref[...], preferred_element_type=F32)
    oc = (z[:, :S5_W] * jax.nn.sigmoid(z[:, S5_W:])).astype(BF16)
    outs = (oa_ref[0], ob_ref[0], oc)
    y = None
    for i in range(N_BRANCH):
        t = gate_ref[0, :, i * D_MODEL:(i + 1) * D_MODEL].astype(F32) * jnp.dot(
            outs[i], wbr_ref[i], preferred_element_type=F32)
        y = t if y is None else y + t
    y2 = jnp.dot(y.astype(BF16), wout_ref[...], preferred_element_type=F32)
    xn = x_ref[0] + _tile_mod(mod_ref, 2, j) * y2
    xo_ref[0] = xn
    h2 = _rms(xn, g2_ref[...]) * (1.0 + _tile_mod(mod_ref, 4, j)) + _tile_mod(mod_ref, 3, j)
    hi, lo = _split_bf16(h2)
    h2_ref[0] = hi
    lg_ref[0] = (jnp.dot(hi, wrh_ref[...], preferred_element_type=F32)
                 + jnp.dot(lo, wrh_ref[...], preferred_element_type=F32)
                 + jnp.dot(hi, wrl_ref[...], preferred_element_type=F32) + br_ref[...])


def _merge(xs, mods, oa, ob, ys, u, gates, d_skip, wglu, wbr, wout, g2, wrh, wrl, br):
    B, S, D = xs.shape
    nt = S // TM
    tok = lambda w: pl.BlockSpec((1, TM, w), lambda b, j: (b, j, 0))
    return pl.pallas_call(
        _merge_kernel,
        out_shape=[jax.ShapeDtypeStruct((B, S, D), F32), jax.ShapeDtypeStruct((B, S, D), BF16),
                   jax.ShapeDtypeStruct((B, S, 128), F32)],
        grid=(B, nt),
        in_specs=[tok(D),
                  pl.BlockSpec((1, 2, 6, D), lambda b, j: (b, 0, 0, 0)),
                  tok(BRANCH_W), tok(BRANCH_W), tok(S5_W), tok(S5_W), tok(N_BRANCH * D_MODEL),
                  _const_spec((1, S5_W)), _const_spec(wglu.shape), _const_spec(wbr.shape),
                  _const_spec(wout.shape), _const_spec((1, D)), _const_spec(wrh.shape),
                  _const_spec(wrl.shape), _const_spec((1, 128))],
        out_specs=[tok(D), tok(D), tok(128)],
        compiler_params=_cparams("parallel", "arbitrary"),
        name="merge_router",
    )(xs, mods, oa, ob, ys, u, gates, d_skip, wglu, wbr, wout, g2, wrh, wrl, br)


def _expert_kernel(be_ref, nu_ref, x_ref, wup_ref, wdn_ref, y_ref, wup_scr, wdn_scr):
    i = pl.program_id(0)
    prev = be_ref[jnp.maximum(i - 1, 0)]

    @pl.when((i == 0) | (be_ref[i] != prev))
    def _():
        wup_scr[...] = wup_ref[0].astype(BF16)
        wdn_scr[...] = wdn_ref[0].astype(BF16)

    @pl.when(i < nu_ref[0])
    def _():
        a = jnp.dot(x_ref[...], wup_scr[...], preferred_element_type=F32)
        g = a[:, :EXPERT_HIDDEN]
        hid = (g * jax.nn.sigmoid(g) * a[:, EXPERT_HIDDEN:]).astype(BF16)
        y_ref[...] = jnp.dot(hid, wdn_scr[...], preferred_element_type=F32).astype(BF16)

    @pl.when(i >= nu_ref[0])
    def _():
        y_ref[...] = jnp.zeros(y_ref.shape, BF16)


def _experts(xg, blk_e, n_used, w_up, w_down):
    n_rows, D = xg.shape
    n_blk = n_rows // MOE_BLOCK
    gs = pltpu.PrefetchScalarGridSpec(
        num_scalar_prefetch=2,
        grid=(n_blk,),
        in_specs=[pl.BlockSpec((MOE_BLOCK, D), lambda i, be, nu: (i, 0)),
                  pl.BlockSpec((1, D, 2 * EXPERT_HIDDEN), lambda i, be, nu: (be[i], 0, 0)),
                  pl.BlockSpec((1, EXPERT_HIDDEN, D), lambda i, be, nu: (be[i], 0, 0))],
        out_specs=pl.BlockSpec((MOE_BLOCK, D), lambda i, be, nu: (i, 0)),
        scratch_shapes=[pltpu.VMEM((D, 2 * EXPERT_HIDDEN), BF16), pltpu.VMEM((EXPERT_HIDDEN, D), BF16)],
    )
    return pl.pallas_call(
        _expert_kernel,
        out_shape=jax.ShapeDtypeStruct((n_rows, D), BF16),
        grid_spec=gs,
        compiler_params=_cparams("arbitrary"),
        name="moe_experts",
    )(blk_e, n_used, xg, w_up, w_down)


def _route(logits):
    gl = logits[:, :N_GROUPS]
    grp = jnp.argmax(gl, axis=-1)
    p_grp = 1.0 / jnp.sum(jnp.exp(gl - jnp.max(gl, axis=-1, keepdims=True)), axis=-1)
    el = logits[:, N_GROUPS:N_GROUPS + N_EXPERTS].reshape(-1, N_GROUPS, EXPERTS_PER_GROUP)
    el = jnp.take_along_axis(el, grp[:, None, None], axis=1)[:, 0]
    top_v, top_i = lax.top_k(el, TOP_K)
    gate = p_grp[:, None] * jax.nn.softmax(top_v, axis=-1)
    eid = grp[:, None].astype(jnp.int32) * EXPERTS_PER_GROUP + top_i.astype(jnp.int32)
    return eid, gate


def _moe(h2, logits, tok_rows, w_up, w_down):
    T = tok_rows.shape[0]
    eid, gate = _route(logits)
    S2 = T * TOP_K
    e_flat = eid.reshape(S2)
    onehot = (e_flat[:, None] == jnp.arange(N_EXPERTS, dtype=jnp.int32)[None, :]).astype(jnp.int32)
    csum = jnp.cumsum(onehot, axis=0)
    rank = jnp.take_along_axis(csum, e_flat[:, None], axis=1)[:, 0] - 1
    counts = csum[-1]
    padded = (counts + MOE_BLOCK - 1) // MOE_BLOCK * MOE_BLOCK
    p_end = jnp.cumsum(padded)
    p_start = p_end - padded
    dest = p_start[e_flat] + rank
    n_blk = -(-(S2 + N_EXPERTS * (MOE_BLOCK - 1)) // MOE_BLOCK)
    row_flat = jnp.repeat(tok_rows.astype(jnp.int32), TOP_K)
    buf_row = jnp.zeros((n_blk * MOE_BLOCK,), jnp.int32).at[dest].set(
        row_flat, mode="promise_in_bounds", unique_indices=True)
    blk_start = jnp.arange(n_blk, dtype=jnp.int32) * MOE_BLOCK
    blk_e = jnp.minimum(jnp.sum((p_end[None, :] <= blk_start[:, None]).astype(jnp.int32), axis=1),
                        N_EXPERTS - 1).astype(jnp.int32)
    n_used = (p_end[-1] // MOE_BLOCK).astype(jnp.int32).reshape(1)
    xg = h2.at[buf_row].get(mode="promise_in_bounds")
    y = _experts(xg, blk_e, n_used, w_up, w_down)
    pos = dest.reshape(T, TOP_K)
    return (gate[:, 0:1] * y.at[pos[:, 0]].get(mode="promise_in_bounds").astype(F32)
            + gate[:, 1:2] * y.at[pos[:, 1]].get(mode="promise_in_bounds").astype(F32))


def _rope_swap_index():
    i = np.arange(MLA_ROPE)
    return (i // 16) * 16 + (1 - (i % 16) // 8) * 8 + i % 8


def _pack_w_in(w_in):
    sw = _rope_swap_index()
    kr = w_in[:, C_KR:C_NK]
    z64 = jnp.zeros((w_in.shape[0], 64), w_in.dtype)
    return jnp.concatenate([w_in[:, C_KV:C_KR], z64, kr, kr[:, sw], w_in[:, C_NK:]], axis=1).astype(BF16)


def _pack_w_uq(w_uq):
    sw = _rope_swap_index()
    w = w_uq.reshape(Q_LORA, HEADS, MLA_QK)
    return jnp.concatenate([w, w[:, :, MLA_NOPE:][:, :, sw]], axis=-1).reshape(Q_LORA, HEADS * HP).astype(BF16)


def _pack_w_ukv(w_ukv):
    w = w_ukv.reshape(KV_LORA, HEADS, MLA_NOPE + MLA_V)
    z = jnp.zeros((KV_LORA, HEADS, 64), w.dtype)
    kn = jnp.concatenate([w[:, :, :MLA_NOPE], z], axis=-1)
    v = w[:, :, MLA_NOPE:]
    even = (np.arange(HEADS) % 2 == 0)[None, :, None]
    vp = jnp.concatenate([jnp.where(even, v, 0.0), jnp.where(even, 0.0, v)], axis=-1)
    return jnp.concatenate([kn.reshape(KV_LORA, HEADS * HP), vp.reshape(KV_LORA, HEADS * HP)], axis=1).astype(BF16)


def _rope_lane_tables(L):
    t = np.arange(L)
    nf = MLA_ROPE // 4
    inv = ROPE_THETA ** (-np.arange(nf, dtype=np.float64) / nf)
    ang = np.concatenate([(t // GRID_W)[:, None] * inv, (t % GRID_W)[:, None] * inv], axis=-1)
    i = np.arange(MLA_ROPE)
    col = (i // 16) * nf + i % 8
    sgn = np.where((i % 16) // 8 == 0, -1.0, 1.0)
    cos = np.concatenate([np.ones((CTX, MLA_ROPE)), np.cos(ang)[:, col]], axis=0)
    sin = np.concatenate([np.zeros((CTX, MLA_ROPE)), np.sin(ang)[:, col] * sgn], axis=0)
    return cos.astype(np.float32), sin.astype(np.float32)


def _head_lane_weights(g, cos, sin, scale):
    sw = _rope_swap_index()
    S = cos.shape[0]
    g = g.astype(F32) * scale
    w1 = jnp.concatenate([jnp.broadcast_to(g[:MLA_NOPE], (S, MLA_NOPE)), g[MLA_NOPE:] * cos,
                          jnp.zeros((S, 32), F32)], axis=1)
    w2 = jnp.concatenate([jnp.zeros((S, MLA_NOPE), F32), g[MLA_NOPE:][sw] * sin, jnp.zeros((S, 32), F32)], axis=1)
    return w1, w2


def kernel(x, c, ctx, c_ctx, w_ada, b_ada, g_norm1, w_in, g_q_lat, w_uq, g_kv_lat, w_ukv, g_qn_mla, g_kn_mla,
           g_qn_nat, g_kn_nat, nat_rpb, s5_a_re, s5_a_im, s5_log_dt, s5_b_re, s5_b_im, s5_c_re, s5_c_im, s5_d,
           w_glu, w_branch, w_out, g_norm2, w_group, b_group, w_expert, b_expert, w_up, w_down):
    B, L, D = x.shape
    S = CTX + L
    assert ctx.shape[1] == CTX and D == D_MODEL and L % GRID_W == 0 and S % TM == 0
    depth = w_in.shape[0]
    rows = L // GRID_W
    cos, sin = _rope_lane_tables(L)
    rows_pad = -(-(B + 1) // 8) * 8
    c_all = jnp.concatenate([c, c_ctx[None], jnp.zeros((rows_pad - B - 1, D), c.dtype)], axis=0).astype(F32)
    row_id = jnp.arange(B * S, dtype=jnp.int32).reshape(B, S)

    xs = jnp.concatenate([ctx, x], axis=1).astype(F32)
    for l in range(depth):
        need_ctx = l < depth - 1
        ada = _ada(c_all, w_ada[l], b_ada[l])
        mods = jnp.stack([jnp.broadcast_to(ada[B].reshape(1, 6, D), (B, 6, D)), ada[:B].reshape(B, 6, D)], axis=1)

        kw1, kw2 = _head_lane_weights(g_kn_mla[l], cos, sin, 1.0)
        qw1, qw2 = _head_lane_weights(g_qn_mla[l], cos, sin, MLA_QK ** -0.5)
        gnq = jnp.tile(g_qn_nat[l].astype(F32) * NAT_DIM ** -0.5, 2).reshape(1, 128)
        gnk = jnp.tile(g_kn_nat[l].astype(F32), 2).reshape(1, 128)
        qm, km, vm, qn, kn, vn, u, gates = _front(
            xs, mods, g_norm1[l].reshape(1, D).astype(F32), _pack_w_in(w_in[l]),
            g_kv_lat[l].reshape(1, KV_LORA).astype(F32), _pack_w_ukv(w_ukv[l]),
            g_q_lat[l].reshape(1, Q_LORA).astype(F32), _pack_w_uq(w_uq[l]), kw1, kw2, qw1, qw2, gnq, gnk)

        oa = _mla(qm, km, vm, need_ctx)
        ob = _nat(qn, kn, vn, _nat_bias_table(nat_rpb[l], rows), need_ctx)
        ys = _s5(u, _s5_mats(s5_a_re[l], s5_a_im[l], s5_log_dt[l], s5_b_re[l], s5_b_im[l], s5_c_re[l], s5_c_im[l]))

        w_route = jnp.concatenate([w_group[l], w_expert[l], jnp.zeros((D, 128 - N_GROUPS - N_EXPERTS), F32)],
                                  axis=1).astype(F32)
        wrh, wrl = _split_bf16(w_route)
        b_route = jnp.concatenate([b_group[l], b_expert[l], jnp.zeros((128 - N_GROUPS - N_EXPERTS,), F32)]).reshape(1, 128)
        x_mid, h2, logits = _merge(
            xs, mods, oa, ob, ys, u, gates, s5_d[l].reshape(1, S5_W).astype(F32), w_glu[l].astype(BF16),
            w_branch[l].astype(BF16), w_out[l].astype(BF16), g_norm2[l].reshape(1, D).astype(F32), wrh, wrl,
            b_route.astype(F32))

        tok_rows = (row_id if need_ctx else row_id[:, CTX:]).reshape(-1)
        lg = (logits if need_ctx else logits[:, CTX:]).reshape(-1, 128)
        f = _moe(h2.reshape(B * S, D), lg, tok_rows, w_up[l], w_down[l])
        gt2 = mods[:, 1, 5][:, None, :]
        if need_ctx:
            gt2_all = jnp.concatenate([jnp.broadcast_to(mods[:, 0, 5][:, None, :], (B, CTX, D)),
                                       jnp.broadcast_to(gt2, (B, L, D))], axis=1)
            xs = x_mid + gt2_all * f.reshape(B, S, D)
        else:
            xs = x_mid[:, CTX:] + gt2 * f.reshape(B, L, D)
    return xs.astype(x.dtype)
```

```python
import functools

import numpy as np
import jax
import jax.numpy as jnp
from jax import lax
from jax.experimental import pallas as pl
from jax.experimental.pallas import tpu as pltpu

F32 = jnp.float32
BF16 = jnp.bfloat16
HIGHEST = lax.Precision.HIGHEST

D_MODEL = 1024
GRID_W = 64
N_BRANCH = 3
BRANCH_W = 512
RMS_EPS = 1e-6
NEG = -1e30
HEADS = 8
MLA_NOPE = 64
MLA_ROPE = 32
MLA_QK = 96
MLA_V = 64
KV_LORA = 256
Q_LORA = 768
ROPE_THETA = 10000.0
NAT_DIM = 64
WIN_H = 8
WIN_W = 16
S5_W = 512
S5_GROUP = 16
S5_GROUPS = 32
S5_STATE = 64
N_GROUPS = 4
EXPERTS_PER_GROUP = 8
N_EXPERTS = 32
EXPERT_HIDDEN = 512
TOP_K = 2

CTX = 256
TM = 768
S5_CHUNK = 16
TC = TM // S5_CHUNK
RT = 256
MOE_BLOCK = 512
HP = 128
NAT_QROWS = 4
NAT_KROWS = NAT_QROWS + WIN_H
VMEM_LIMIT = 56 * 1024 * 1024
VMEM_LIMIT_FRONT = 60 * 1024 * 1024

P_KV = 0
P_KR = P_KV + KV_LORA
P_NK = P_KR + HP
P_NV = P_NK + 512
P_S5 = P_NV + 512
P_QL = P_S5 + 512
P_NQ = P_QL + Q_LORA
P_GATE = P_NQ + 512
P_END = P_GATE + N_BRANCH * D_MODEL

C_KV = 0
C_KR = C_KV + KV_LORA
C_NK = C_KR + MLA_ROPE
C_NV = C_NK + 512
C_S5 = C_NV + 512
C_QL = C_S5 + S5_W
C_NQ = C_QL + Q_LORA
C_GATE = C_NQ + 512


def _cparams(*sem, vmem=VMEM_LIMIT):
    return pltpu.CompilerParams(dimension_semantics=sem, vmem_limit_bytes=vmem)


def _const_spec(shape):
    nd = len(shape)
    return pl.BlockSpec(shape, lambda *_: (0,) * nd, pipeline_mode=pl.Buffered(1))


def _ada_kernel(c_ref, w_ref, b_ref, o_ref):
    c = c_ref[...]
    s = (c * jax.nn.sigmoid(c)).astype(BF16)
    o_ref[...] = jnp.dot(s, w_ref[...].astype(BF16), preferred_element_type=F32) + b_ref[...]


def _ada(c_all, w_ada, b_ada):
    rows, d = c_all.shape
    n = w_ada.shape[1]
    bn = 512
    return pl.pallas_call(
        _ada_kernel,
        out_shape=jax.ShapeDtypeStruct((rows, n), F32),
        grid=(n // bn,),
        in_specs=[pl.BlockSpec((rows, d), lambda i: (0, 0)),
                  pl.BlockSpec((d, bn), lambda i: (0, i)),
                  pl.BlockSpec((1, bn), lambda i: (0, i))],
        out_specs=pl.BlockSpec((rows, bn), lambda i: (0, i)),
        compiler_params=_cparams("arbitrary"),
        name="ada_mod",
    )(c_all, w_ada, b_ada.reshape(1, n))


def _tile_mod(mod_ref, idx, tile):
    ctx_v = mod_ref[0, 0, idx:idx + 1, :]
    lat_v = mod_ref[0, 1, idx:idx + 1, :]
    row = lax.broadcasted_iota(jnp.int32, (TM, 1), 0) + tile * TM
    return jnp.where(row < CTX, ctx_v, lat_v)


def _rms(x, g):
    return x * lax.rsqrt(jnp.mean(x * x, axis=-1, keepdims=True) + RMS_EPS) * g


def _mla_head(xh, w1, w2):
    lane = lax.broadcasted_iota(jnp.int32, xh.shape, 1)
    sq = jnp.where(lane < MLA_QK, xh * xh, 0.0)
    inv = lax.rsqrt(jnp.sum(sq, axis=-1, keepdims=True) * (1.0 / MLA_QK) + RMS_EPS)
    return inv * (xh * w1 + pltpu.roll(xh, HP - MLA_ROPE, 1) * w2)


def _pair_norm(x2, g2):
    lane = lax.broadcasted_iota(jnp.int32, x2.shape, 1)
    lo = lane < NAT_DIM
    sq = x2 * x2
    s_lo = jnp.sum(jnp.where(lo, sq, 0.0), axis=-1, keepdims=True)
    s_hi = jnp.sum(jnp.where(lo, 0.0, sq), axis=-1, keepdims=True)
    inv = lax.rsqrt(jnp.where(lo, s_lo, s_hi) * (1.0 / NAT_DIM) + RMS_EPS)
    return x2 * inv * g2


def _lane_group(shape):
    return lax.broadcasted_iota(jnp.int32, shape, 1) // S5_GROUP


def _chunk_shuffle(u_scr, ug_ref):
    grp = _lane_group((TC, 128))
    for q in range(S5_W // 128):
        for hf in range(2):
            tiles = [u_scr[q, pl.ds(8 * hf + sp, TC, stride=S5_CHUNK), :] for sp in range(8)]
            for gg in range(8):
                acc = None
                for sp in range(8):
                    sh = ((sp - gg) * S5_GROUP) % 128
                    r = tiles[sp] if sh == 0 else pltpu.roll(tiles[sp], sh, 1)
                    acc = r if acc is None else jnp.where(grp == sp, r, acc)
                ug_ref[0, 8 * q + gg, :, hf * 128:(hf + 1) * 128] = acc.astype(BF16)


def _chunk_unshuffle(yg_ref, y_scr):
    grp = _lane_group((TC, 128))
    for q in range(S5_W // 128):
        for hf in range(2):
            tiles = [yg_ref[0, 8 * q + gg, :, hf * 128:(hf + 1) * 128].astype(F32) for gg in range(8)]
            for tp in range(8):
                acc = None
                for gg in range(8):
                    sh = ((gg - tp) * S5_GROUP) % 128
                    r = tiles[gg] if sh == 0 else pltpu.roll(tiles[gg], sh, 1)
                    acc = r if acc is None else jnp.where(grp == gg, r, acc)
                y_scr[q, pl.ds(8 * hf + tp, TC, stride=S5_CHUNK), :] = acc


def _front_kernel(x_ref, mod_ref, g1_ref, win_ref, gkv_ref, wukv_ref, gq_ref, wuq_ref,
                  kw1_ref, kw2_ref, qw1_ref, qw2_ref, gnq_ref, gnk_ref,
                  qm_ref, km_ref, vm_ref, qn_ref, kn_ref, vn_ref, ug_ref, gate_ref, u_scr):
    j = pl.program_id(1)
    sh = _tile_mod(mod_ref, 0, j)
    sc = _tile_mod(mod_ref, 1, j)
    h = (_rms(x_ref[0], g1_ref[...]) * (1.0 + sc) + sh).astype(BF16)

    def proj(a, b):
        return jnp.dot(h, win_ref[:, a:b], preferred_element_type=F32)

    kvl = _rms(proj(P_KV, P_KR), gkv_ref[...]).astype(BF16)
    krp = proj(P_KR, P_NK)
    kw1 = kw1_ref[...]
    kw2 = kw2_ref[...]
    for hd in range(HEADS):
        kn = jnp.dot(kvl, wukv_ref[:, hd * HP:(hd + 1) * HP], preferred_element_type=F32)
        km_ref[0, :, hd * HP:(hd + 1) * HP] = _mla_head(kn + krp, kw1, kw2).astype(BF16)
    vm_ref[0] = jnp.dot(kvl, wukv_ref[:, HEADS * HP:], preferred_element_type=F32).astype(BF16)

    ql = _rms(proj(P_QL, P_NQ), gq_ref[...]).astype(BF16)
    qw1 = qw1_ref[...]
    qw2 = qw2_ref[...]
    for hd in range(HEADS):
        qh = jnp.dot(ql, wuq_ref[:, hd * HP:(hd + 1) * HP], preferred_element_type=F32)
        qm_ref[0, :, hd * HP:(hd + 1) * HP] = _mla_head(qh, qw1, qw2).astype(BF16)

    gnq = gnq_ref[...]
    gnk = gnk_ref[...]
    for c in range(4):
        kk = proj(P_NK + c * 128, P_NK + (c + 1) * 128)
        kn_ref[0, :, c * 128:(c + 1) * 128] = _pair_norm(kk, gnk).astype(BF16)
        qq = proj(P_NQ + c * 128, P_NQ + (c + 1) * 128)
        qn_ref[0, :, c * 128:(c + 1) * 128] = _pair_norm(qq, gnq).astype(BF16)
    vn_ref[0] = proj(P_NV, P_S5).astype(BF16)

    for q in range(S5_W // 128):
        u_scr[q] = proj(P_S5 + q * 128, P_S5 + (q + 1) * 128)
    _chunk_shuffle(u_scr, ug_ref)
    for c in range(N_BRANCH * D_MODEL // 512):
        gate_ref[0, :, c * 512:(c + 1) * 512] = jax.nn.sigmoid(
            proj(P_GATE + c * 512, P_GATE + (c + 1) * 512)).astype(BF16)


def _front(xs, mods, g1, w_in_p, gkv, wukv_p, gq, wuq_p, kw1, kw2, qw1, qw2, gnq, gnk):
    B, S, D = xs.shape
    nt = S // TM
    tok = lambda w: pl.BlockSpec((1, TM, w), lambda b, j: (b, j, 0))
    tab = pl.BlockSpec((TM, HP), lambda b, j: (j, 0))
    outs = [HEADS * HP, HEADS * HP, HEADS * HP, 512, 512, 512, None, N_BRANCH * D_MODEL]
    cw = S5_CHUNK * S5_GROUP
    ug_shape = jax.ShapeDtypeStruct((B, S5_GROUPS, S // S5_CHUNK, cw), BF16)
    ug_spec = pl.BlockSpec((1, S5_GROUPS, TC, cw), lambda b, j: (b, 0, j, 0))
    return pl.pallas_call(
        _front_kernel,
        out_shape=[ug_shape if w is None else jax.ShapeDtypeStruct((B, S, w), BF16) for w in outs],
        grid=(B, nt),
        in_specs=[tok(D),
                  pl.BlockSpec((1, 2, 6, D), lambda b, j: (b, 0, 0, 0)),
                  _const_spec((1, D)), _const_spec(w_in_p.shape), _const_spec((1, KV_LORA)),
                  _const_spec(wukv_p.shape), _const_spec((1, Q_LORA)), _const_spec(wuq_p.shape),
                  tab, tab, tab, tab, _const_spec((1, 128)), _const_spec((1, 128))],
        out_specs=[ug_spec if w is None else tok(w) for w in outs],
        scratch_shapes=[pltpu.VMEM((S5_W // 128, TM, 128), F32)],
        compiler_params=_cparams("parallel", "arbitrary", vmem=VMEM_LIMIT_FRONT),
        name="mixer_front",
    )(xs, mods, g1, w_in_p, gkv, wukv_p, gq, wuq_p, kw1, kw2, qw1, qw2, gnq, gnk)


def _nt(a, b):
    return lax.dot_general(a, b, (((1,), (1,)), ((), ())), preferred_element_type=F32)


def _softmax_pv(s, v):
    m = jnp.max(s, axis=-1, keepdims=True)
    p = jnp.exp(s - m)
    l = jnp.sum(p, axis=-1, keepdims=True)
    return jnp.dot(p.astype(BF16), v, preferred_element_type=F32) * (1.0 / l)


def _mla_kernel(q_ref, k_ref, v_ref, o_ref, *, seq, need_ctx):
    j = pl.program_id(2)

    def attend(r0, nr, nk):
        o = None
        for hh in range(2):
            sl = slice(hh * HP, (hh + 1) * HP)
            s = _nt(q_ref[0, r0:r0 + nr, sl], k_ref[0, :nk, sl])
            t = _softmax_pv(s, v_ref[0, :nk, sl])
            o = t if o is None else o + t
        o_ref[0, r0:r0 + nr, :] = o.astype(BF16)

    @pl.when(j == 0)
    def _():
        if need_ctx:
            attend(0, CTX, CTX)
        else:
            o_ref[0, 0:CTX, :] = jnp.zeros((CTX, 128), BF16)
        attend(CTX, TM - CTX, seq)

    @pl.when(j > 0)
    def _():
        attend(0, TM, seq)


def _mla(qm, km, vm, need_ctx):
    B, S, _ = qm.shape
    nt = S // TM
    return pl.pallas_call(
        functools.partial(_mla_kernel, seq=S, need_ctx=need_ctx),
        out_shape=jax.ShapeDtypeStruct((B, S, BRANCH_W), BF16),
        grid=(B, HEADS // 2, nt),
        in_specs=[pl.BlockSpec((1, TM, 2 * HP), lambda b, h, j: (b, j, h)),
                  pl.BlockSpec((1, S, 2 * HP), lambda b, h, j: (b, 0, h)),
                  pl.BlockSpec((1, S, 2 * HP), lambda b, h, j: (b, 0, h))],
        out_specs=pl.BlockSpec((1, TM, 128), lambda b, h, j: (b, j, h)),
        compiler_params=_cparams("parallel", "parallel", "arbitrary"),
        name="mla_attn",
    )(qm, km, vm)


def _nat_kernel(q_ref, k_ref, v_ref, bias_ref, o_ref, *, rows, need_ctx):
    lane = lax.broadcasted_iota(jnp.int32, (1, 128), 1)
    lo = lane < NAT_DIM
    masks = (lo, jnp.logical_not(lo))
    nq = NAT_QROWS * GRID_W
    nk = NAT_KROWS * GRID_W
    n_blocks = rows // NAT_QROWS

    def half(x, hh):
        return jnp.where(masks[hh], x, jnp.zeros_like(x))

    kc = k_ref[0, 0:CTX, :]
    vc = v_ref[0, 0:CTX, :]

    if need_ctx:
        qc = q_ref[0, 0:CTX, :]
        o = None
        for hh in range(2):
            t = _softmax_pv(_nt(half(qc, hh), kc), half(vc, hh))
            o = t if o is None else o + t
        o_ref[0, 0:CTX, :] = o.astype(BF16)
    else:
        o_ref[0, 0:CTX, :] = jnp.zeros((CTX, 128), BF16)

    def block(i, carry):
        ws = jnp.clip(i * NAT_QROWS - WIN_H // 2, 0, rows - NAT_KROWS)
        cls = jnp.where(i == 0, 0, jnp.where(i == n_blocks - 1, 2, 1))
        q0 = pl.multiple_of(CTX + i * nq, nq)
        k0 = pl.multiple_of(CTX + ws * GRID_W, GRID_W)
        q = q_ref[0, pl.ds(q0, nq), :]
        kw = k_ref[0, pl.ds(k0, nk), :]
        vw = v_ref[0, pl.ds(k0, nk), :]
        o = None
        for hh in range(2):
            qh = half(q, hh)
            s_lat = _nt(qh, kw) + bias_ref[0, hh, cls]
            s_ctx = _nt(qh, kc)
            m = jnp.maximum(jnp.max(s_lat, axis=-1, keepdims=True), jnp.max(s_ctx, axis=-1, keepdims=True))
            p_lat = jnp.exp(s_lat - m)
            p_ctx = jnp.exp(s_ctx - m)
            l = jnp.sum(p_lat, axis=-1, keepdims=True) + jnp.sum(p_ctx, axis=-1, keepdims=True)
            acc = (jnp.dot(p_lat.astype(BF16), half(vw, hh), preferred_element_type=F32)
                   + jnp.dot(p_ctx.astype(BF16), half(vc, hh), preferred_element_type=F32))
            t = acc * (1.0 / l)
            o = t if o is None else o + t
        o_ref[0, pl.ds(q0, nq), :] = o.astype(BF16)
        return carry

    lax.fori_loop(0, n_blocks, block, 0)


def _nat(qn, kn, vn, bias, need_ctx):
    B, S, _ = qn.shape
    rows = (S - CTX) // GRID_W
    seq = pl.BlockSpec((1, S, 128), lambda h, b: (b, 0, h))
    return pl.pallas_call(
        functools.partial(_nat_kernel, rows=rows, need_ctx=need_ctx),
        out_shape=jax.ShapeDtypeStruct((B, S, BRANCH_W), BF16),
        grid=(HEADS // 2, B),
        in_specs=[seq, seq, seq,
                  pl.BlockSpec((1, 2, 3, NAT_QROWS * GRID_W, NAT_KROWS * GRID_W), lambda h, b: (h, 0, 0, 0, 0))],
        out_specs=seq,
        compiler_params=_cparams("parallel", "arbitrary"),
        name="nat_attn",
    )(qn, kn, vn, bias)


def _nat_bias_table(rpb, rows):
    assert rows % NAT_QROWS == 0 and rows >= NAT_KROWS
    n_blocks = rows // NAT_QROWS
    cols = np.arange(GRID_W)
    cs = np.clip(cols - WIN_W // 2, 0, GRID_W - WIN_W)
    kcol = cols[None, :]
    qcol = cols[:, None]
    vcol = (kcol >= cs[:, None]) & (kcol < cs[:, None] + WIN_W)
    dcol = np.clip(kcol - qcol + WIN_W - 1, 0, 2 * WIN_W - 2)
    blk = np.array([0, 1, n_blocks - 1])
    ws = np.clip(blk * NAT_QROWS - WIN_H // 2, 0, rows - NAT_KROWS)
    r = blk[:, None] * NAT_QROWS + np.arange(NAT_QROWS)[None, :]
    rs = np.clip(r - WIN_H // 2, 0, rows - WIN_H)
    krow = ws[:, None, None] + np.arange(NAT_KROWS)[None, None, :]
    vrow = (krow >= rs[:, :, None]) & (krow < rs[:, :, None] + WIN_H)
    drow = np.clip(krow - r[:, :, None] + WIN_H - 1, 0, 2 * WIN_H - 2)
    t = rpb.astype(F32)[:, drow]
    t = t[..., dcol]
    valid = vrow[:, :, :, None, None] & vcol[None, None, None]
    t = jnp.where(valid[None], t, NEG)
    t = t.transpose(0, 1, 2, 4, 3, 5).reshape(HEADS, 3, NAT_QROWS * GRID_W, NAT_KROWS * GRID_W)
    return t.reshape(HEADS // 2, 2, 3, NAT_QROWS * GRID_W, NAT_KROWS * GRID_W)


def _s5_kernel(u_ref, mt_ref, f_ref, e_ref, a_ref, d_ref, y_ref, s_scr, hin_scr, *, batch, n_ctx, n_all):
    u = u_ref[0]
    for d in range(2):
        s_scr[d] = jnp.dot(u, f_ref[d, 0], preferred_element_type=F32)

    def make_step(d):
        a1 = a_ref[d, 0, 0:1, :]
        a2 = a_ref[d, 0, 1:2, :]
        a3 = a_ref[d, 0, 2:3, :]

        def step(c, hh):
            h, hs = hh
            r0 = pl.multiple_of(c * batch, batch)
            hin_scr[d, pl.ds(r0, batch), :] = h
            s = s_scr[d, pl.ds(r0, batch), :]
            return (h * a1 + hs * a2 + s[:, :128], hs * a1 + h * a3 + s[:, 128:])
        return step

    z = jnp.zeros((batch, 128), F32)
    lax.fori_loop(0, n_all, make_step(0), (z, z))
    rstep = make_step(1)
    hh = lax.fori_loop(0, n_ctx, lambda i, c: rstep(n_ctx - 1 - i, c), (z, z))
    lax.fori_loop(0, n_all - n_ctx, lambda i, c: rstep(n_all - 1 - i, c), hh)

    y = jnp.dot(u, mt_ref[0], preferred_element_type=F32) + u.astype(F32) * d_ref[0]
    for d in range(2):
        y = y + jnp.dot(hin_scr[d].astype(BF16), e_ref[d, 0], preferred_element_type=F32)
    y_ref[0] = y.astype(BF16)


def _s5(ug, mats, d_skip):
    mt, fcat, e, a = mats
    B, _, C, w = ug.shape
    ug = ug.transpose(1, 2, 0, 3).reshape(S5_GROUPS, C * B, w)
    dg = jnp.tile(d_skip.astype(F32).reshape(S5_GROUPS, 1, S5_GROUP), (1, 1, S5_CHUNK))
    yg = pl.pallas_call(
        functools.partial(_s5_kernel, batch=B, n_ctx=CTX // S5_CHUNK, n_all=C),
        out_shape=jax.ShapeDtypeStruct((S5_GROUPS, C * B, w), BF16),
        grid=(S5_GROUPS,),
        in_specs=[pl.BlockSpec((1, C * B, w), lambda g: (g, 0, 0)),
                  pl.BlockSpec((1, w, w), lambda g: (g, 0, 0)),
                  pl.BlockSpec((2, 1, w, 256), lambda g: (0, g, 0, 0)),
                  pl.BlockSpec((2, 1, 128, w), lambda g: (0, g, 0, 0)),
                  pl.BlockSpec((2, 1, 3, 128), lambda g: (0, g, 0, 0)),
                  pl.BlockSpec((1, 1, w), lambda g: (g, 0, 0))],
        out_specs=pl.BlockSpec((1, C * B, w), lambda g: (g, 0, 0)),
        scratch_shapes=[pltpu.VMEM((2, C * B, 256), F32), pltpu.VMEM((2, C * B, 128), F32)],
        compiler_params=_cparams("parallel"),
        name="s5_scan",
    )(ug, mt, fcat, e, a, dg)
    return yg.reshape(S5_GROUPS, C, B, w).transpose(2, 0, 1, 3)


def _s5_mats(a_re, a_im, log_dt, b_re, b_im, c_re, c_im):
    L = S5_CHUNK
    lam_re = jnp.minimum(a_re.astype(F32), -1e-4)
    lam_im = a_im.astype(F32)
    dt = jnp.exp(log_dt.astype(F32))[..., None]
    mag = jnp.exp(lam_re * dt)
    ab_re = mag * jnp.cos(lam_im * dt)
    ab_im = mag * jnp.sin(lam_im * dt)
    den = lam_re * lam_re + lam_im * lam_im
    n_re = ab_re - 1.0
    f_re = (n_re * lam_re + ab_im * lam_im) / den
    f_im = (ab_im * lam_re - n_re * lam_im) / den
    br = b_re.astype(F32)
    bi = b_im.astype(F32)
    bb_re = f_re[..., None] * br - f_im[..., None] * bi
    bb_im = f_re[..., None] * bi + f_im[..., None] * br
    tau = jnp.arange(L + 1, dtype=F32)
    pm = jnp.exp((lam_re * dt)[..., None] * tau)
    ph = (lam_im * dt)[..., None] * tau
    pw_re = pm * jnp.cos(ph)
    pw_im = pm * jnp.sin(ph)
    cr = c_re.astype(F32)
    ci = c_im.astype(F32)
    pwt_re = pw_re.transpose(0, 1, 3, 2)[:, :, :, None, :]
    pwt_im = pw_im.transpose(0, 1, 3, 2)[:, :, :, None, :]
    ca_re = cr[:, :, None] * pwt_re - ci[:, :, None] * pwt_im
    ca_im = cr[:, :, None] * pwt_im + ci[:, :, None] * pwt_re
    kk = (jnp.einsum('dgtnp,dgpm->dgtnm', ca_re, bb_re, precision=HIGHEST)
          - jnp.einsum('dgtnp,dgpm->dgtnm', ca_im, bb_im, precision=HIGHEST))
    t = np.arange(L)
    lag_f = t[:, None] - t[None, :]
    m_f = jnp.where((lag_f >= 0)[None, :, :, None, None], kk[0][:, np.clip(lag_f, 0, L)], 0.0)
    m_r = jnp.where((lag_f <= 0)[None, :, :, None, None], kk[1][:, np.clip(-lag_f, 0, L)], 0.0)
    mt = (m_f + m_r).transpose(0, 2, 4, 1, 3).reshape(S5_GROUPS, L * S5_GROUP, L * S5_GROUP)
    pow_f = np.stack([L - 1 - t, t])
    pr = jnp.stack([pw_re[d][:, :, pow_f[d]] for d in range(2)])
    pi = jnp.stack([pw_im[d][:, :, pow_f[d]] for d in range(2)])
    fr = pr[..., None] * bb_re[:, :, :, None, :] - pi[..., None] * bb_im[:, :, :, None, :]
    fi = pr[..., None] * bb_im[:, :, :, None, :] + pi[..., None] * bb_re[:, :, :, None, :]
    fr = fr.transpose(0, 1, 3, 4, 2).reshape(2, S5_GROUPS, L * S5_GROUP, S5_STATE)
    fi = fi.transpose(0, 1, 3, 4, 2).reshape(2, S5_GROUPS, L * S5_GROUP, S5_STATE)
    fcat = jnp.concatenate([fr, fi, fi, fr], axis=-1)
    pow_e = np.stack([t + 1, L - t])
    er = jnp.stack([ca_re[d][:, pow_e[d]] for d in range(2)])
    ei = jnp.stack([ca_im[d][:, pow_e[d]] for d in range(2)])
    er = er.transpose(0, 1, 4, 2, 3).reshape(2, S5_GROUPS, S5_STATE, L * S5_GROUP)
    ei = ei.transpose(0, 1, 4, 2, 3).reshape(2, S5_GROUPS, S5_STATE, L * S5_GROUP)
    e = jnp.concatenate([er, -ei], axis=2)
    ar = pw_re[..., L]
    ai = pw_im[..., L]
    a = jnp.stack([jnp.concatenate([ar, ar], -1), jnp.concatenate([-ai, ai], -1),
                   jnp.concatenate([ai, -ai], -1)], axis=2)
    return mt.astype(BF16), fcat.astype(BF16), e.astype(BF16), a


def _split_bf16(v):
    hi = v.astype(BF16)
    return hi, (v - hi.astype(F32)).astype(BF16)


def _merge_kernel(x_ref, mod_ref, oa_ref, ob_ref, yg_ref, gate_ref, wglu_ref, wbr_ref,
                  wout_ref, g2_ref, wrh_ref, wrl_ref, br_ref, xo_ref, h2_ref, lg_ref, y_scr):
    j = pl.program_id(1)
    _chunk_unshuffle(yg_ref, y_scr)
    yc = jnp.concatenate([y_scr[q] for q in range(S5_W // 128)], axis=1)
    z = jnp.dot(jax.nn.gelu(yc).astype(BF16), wglu_ref[...], preferred_element_type=F32)
    oc = (z[:, :S5_W] * jax.nn.sigmoid(z[:, S5_W:])).astype(BF16)
    outs = (oa_ref[0], ob_ref[0], oc)
    y = None
    for i in range(N_BRANCH):
        t = gate_ref[0, :, i * D_MODEL:(i + 1) * D_MODEL].astype(F32) * jnp.dot(
            outs[i], wbr_ref[i], preferred_element_type=F32)
        y = t if y is None else y + t
    y2 = jnp.dot(y.astype(BF16), wout_ref[...], preferred_element_type=F32)
    xn = x_ref[0] + _tile_mod(mod_ref, 2, j) * y2
    xo_ref[0] = xn
    h2 = _rms(xn, g2_ref[...]) * (1.0 + _tile_mod(mod_ref, 4, j)) + _tile_mod(mod_ref, 3, j)
    hi, lo = _split_bf16(h2)
    h2_ref[0] = hi
    lg_ref[0] = (jnp.dot(hi, wrh_ref[...], preferred_element_type=F32)
                 + jnp.dot(lo, wrh_ref[...], preferred_element_type=F32)
                 + jnp.dot(hi, wrl_ref[...], preferred_element_type=F32) + br_ref[...])


def _merge(xs, mods, oa, ob, yg, gates, wglu, wbr, wout, g2, wrh, wrl, br):
    B, S, D = xs.shape
    nt = S // TM
    tok = lambda w: pl.BlockSpec((1, TM, w), lambda b, j: (b, j, 0))
    return pl.pallas_call(
        _merge_kernel,
        out_shape=[jax.ShapeDtypeStruct((B, S, D), F32), jax.ShapeDtypeStruct((B, S, D), BF16),
                   jax.ShapeDtypeStruct((B, S, 128), F32)],
        grid=(B, nt),
        in_specs=[tok(D),
                  pl.BlockSpec((1, 2, 6, D), lambda b, j: (b, 0, 0, 0)),
                  tok(BRANCH_W), tok(BRANCH_W),
                  pl.BlockSpec((1, S5_GROUPS, TC, S5_CHUNK * S5_GROUP), lambda b, j: (b, 0, j, 0)),
                  tok(N_BRANCH * D_MODEL), _const_spec(wglu.shape), _const_spec(wbr.shape),
                  _const_spec(wout.shape), _const_spec((1, D)), _const_spec(wrh.shape),
                  _const_spec(wrl.shape), _const_spec((1, 128))],
        out_specs=[tok(D), tok(D), tok(128)],
        scratch_shapes=[pltpu.VMEM((S5_W // 128, TM, 128), F32)],
        compiler_params=_cparams("parallel", "arbitrary"),
        name="merge_router",
    )(xs, mods, oa, ob, yg, gates, wglu, wbr, wout, g2, wrh, wrl, br)


def _expert_kernel(be_ref, nu_ref, x_ref, wup_ref, wdn_ref, y_ref, wup_scr, wdn_scr):
    i = pl.program_id(0)
    prev = be_ref[jnp.maximum(i - 1, 0)]

    @pl.when((i == 0) | (be_ref[i] != prev))
    def _():
        wup_scr[...] = wup_ref[0, 0].astype(BF16)
        wdn_scr[...] = wdn_ref[0, 0].astype(BF16)

    @pl.when(i < nu_ref[0])
    def _():
        a = jnp.dot(x_ref[...], wup_scr[...], preferred_element_type=F32)
        g = a[:, :EXPERT_HIDDEN]
        hid = (g * jax.nn.sigmoid(g) * a[:, EXPERT_HIDDEN:]).astype(BF16)
        y_ref[...] = jnp.dot(hid, wdn_scr[...], preferred_element_type=F32).astype(BF16)

    @pl.when(i >= nu_ref[0])
    def _():
        y_ref[...] = jnp.zeros(y_ref.shape, BF16)


def _experts(xg, blk_e, n_used, w_up, w_down, layer):
    n_rows, D = xg.shape
    n_blk = n_rows // MOE_BLOCK
    gs = pltpu.PrefetchScalarGridSpec(
        num_scalar_prefetch=2,
        grid=(n_blk,),
        in_specs=[pl.BlockSpec((MOE_BLOCK, D), lambda i, be, nu: (i, 0)),
                  pl.BlockSpec((1, 1, D, 2 * EXPERT_HIDDEN), lambda i, be, nu: (layer, be[i], 0, 0)),
                  pl.BlockSpec((1, 1, EXPERT_HIDDEN, D), lambda i, be, nu: (layer, be[i], 0, 0))],
        out_specs=pl.BlockSpec((MOE_BLOCK, D), lambda i, be, nu: (i, 0)),
        scratch_shapes=[pltpu.VMEM((D, 2 * EXPERT_HIDDEN), BF16), pltpu.VMEM((EXPERT_HIDDEN, D), BF16)],
    )
    return pl.pallas_call(
        _expert_kernel,
        out_shape=jax.ShapeDtypeStruct((n_rows, D), BF16),
        grid_spec=gs,
        compiler_params=_cparams("arbitrary"),
        name="moe_experts",
    )(blk_e, n_used, xg, w_up, w_down)


R_E0, R_E1, R_RANK0, R_RANK1, R_G0, R_G1 = 0, 1, 2, 3, 4, 5


def _router_kernel(lg_ref, tri_ref, rt_ref, cnt_ref, base_scr):
    first = (pl.program_id(0) == 0) & (pl.program_id(1) == 0)

    @pl.when(first)
    def _():
        base_scr[...] = jnp.zeros(base_scr.shape, F32)

    lg = lg_ref[0]
    lane = lax.broadcasted_iota(jnp.int32, lg.shape, 1)
    big = jnp.int32(128)

    def first_max(v):
        m = jnp.max(v, axis=-1, keepdims=True)
        return m, jnp.min(jnp.where(v == m, lane, big), axis=-1, keepdims=True)

    gmask = lane < N_GROUPS
    gmax, grp = first_max(jnp.where(gmask, lg, -jnp.inf))
    p_grp = 1.0 / jnp.sum(jnp.where(gmask, jnp.exp(lg - gmax), 0.0), axis=-1, keepdims=True)
    e_lo = N_GROUPS + grp * EXPERTS_PER_GROUP
    el = jnp.where((lane >= e_lo) & (lane < e_lo + EXPERTS_PER_GROUP), lg, -jnp.inf)
    v1, i1 = first_max(el)
    v2, i2 = first_max(jnp.where(lane == i1, -jnp.inf, el))
    e2 = jnp.exp(v2 - v1)
    g1 = p_grp / (1.0 + e2)
    g2 = g1 * e2
    hot1 = lane == i1
    hot2 = lane == i2
    onehot = jnp.where(hot1 | hot2, 1.0, 0.0)
    cnt = jnp.dot(tri_ref[...], onehot.astype(BF16), preferred_element_type=F32) + base_scr[...]
    r1 = jnp.sum(jnp.where(hot1, cnt, 0.0), axis=-1, keepdims=True)
    r2 = jnp.sum(jnp.where(hot2, cnt, 0.0), axis=-1, keepdims=True)
    base_scr[...] = base_scr[...] + jnp.sum(onehot, axis=0, keepdims=True)
    cnt_ref[...] = base_scr[...]
    out = jnp.zeros(lg.shape, F32)
    for k, v in ((R_E0, (i1 - N_GROUPS).astype(F32)), (R_E1, (i2 - N_GROUPS).astype(F32)),
                 (R_RANK0, r1), (R_RANK1, r2), (R_G0, g1), (R_G1, g2)):
        out = jnp.where(lane == k, v, out)
    rt_ref[0] = out


def _router(logits, tile_off):
    B, S, _ = logits.shape
    nt = S // RT - tile_off
    tri = (np.arange(RT)[None, :] < np.arange(RT)[:, None]).astype(np.float32)
    return pl.pallas_call(
        _router_kernel,
        out_shape=[jax.ShapeDtypeStruct((B, nt * RT, 128), F32), jax.ShapeDtypeStruct((1, 128), F32)],
        grid=(B, nt),
        in_specs=[pl.BlockSpec((1, RT, 128), lambda b, j: (b, j + tile_off, 0)), _const_spec((RT, RT))],
        out_specs=[pl.BlockSpec((1, RT, 128), lambda b, j: (b, j, 0)), pl.BlockSpec((1, 128), lambda b, j: (0, 0))],
        scratch_shapes=[pltpu.VMEM((1, 128), F32)],
        compiler_params=_cparams("arbitrary", "arbitrary"),
        name="moe_router",
    )(logits, jnp.asarray(tri, BF16))


def _combine_kernel(x_ref, mod_ref, rt_ref, y_ref, o_ref):
    rt = rt_ref[0]
    y0 = y_ref[0, :, :D_MODEL].astype(F32)
    y1 = y_ref[0, :, D_MODEL:].astype(F32)
    o_ref[0] = x_ref[0] + mod_ref[0, 0, 5:6, :] * (rt[:, R_G0:R_G0 + 1] * y0 + rt[:, R_G1:R_G1 + 1] * y1)


def _combine(x_mid, mods, rt, y2, tile_off):
    B, S, D = x_mid.shape
    So = rt.shape[1]
    tok = lambda w: pl.BlockSpec((1, RT, w), lambda b, j: (b, j, 0))
    return pl.pallas_call(
        _combine_kernel,
        out_shape=jax.ShapeDtypeStruct((B, So, D), F32),
        grid=(B, So // RT),
        in_specs=[pl.BlockSpec((1, RT, D), lambda b, j: (b, j + tile_off, 0)),
                  pl.BlockSpec((1, 1, 6, D), lambda b, j: (b, jnp.minimum(j + tile_off, 1), 0, 0)),
                  tok(128), tok(2 * D)],
        out_specs=tok(D),
        compiler_params=_cparams("parallel", "arbitrary"),
        name="moe_combine",
    )(x_mid, mods, rt, y2)


def _moe(x_mid, mods, h2, logits, w_up, w_down, layer, tile_off):
    B, S, D = x_mid.shape
    rt, cnt = _router(logits, tile_off)
    So = rt.shape[1]
    T = B * So
    S2 = T * TOP_K
    counts = cnt[0, N_GROUPS:N_GROUPS + N_EXPERTS].astype(jnp.int32)
    padded = (counts + MOE_BLOCK - 1) // MOE_BLOCK * MOE_BLOCK
    p_end = jnp.cumsum(padded)
    p_start = p_end - padded
    rtf = rt.reshape(T, 128)
    eid = rtf[:, R_E0:R_E1 + 1].astype(jnp.int32)
    rank = rtf[:, R_RANK0:R_RANK1 + 1].astype(jnp.int32)
    dest = (jnp.sum(jnp.where(eid[:, :, None] == jnp.arange(N_EXPERTS, dtype=jnp.int32), p_start, 0), axis=-1)
            + rank).reshape(S2)
    n_blk = -(-(S2 + N_EXPERTS * (MOE_BLOCK - 1)) // MOE_BLOCK)
    rows = (jnp.arange(B, dtype=jnp.int32)[:, None] * S + tile_off * RT
            + jnp.arange(So, dtype=jnp.int32)[None, :]).reshape(T)
    buf_row = jnp.zeros((n_blk * MOE_BLOCK,), jnp.int32).at[dest].set(
        jnp.repeat(rows, TOP_K), mode="promise_in_bounds", unique_indices=True)
    blk_start = jnp.arange(n_blk, dtype=jnp.int32) * MOE_BLOCK
    blk_e = jnp.minimum(jnp.sum((p_end[None, :] <= blk_start[:, None]).astype(jnp.int32), axis=1),
                        N_EXPERTS - 1).astype(jnp.int32)
    n_used = (p_end[-1] // MOE_BLOCK).astype(jnp.int32).reshape(1)
    xg = h2.reshape(B * S, D).at[buf_row].get(mode="promise_in_bounds")
    y = _experts(xg, blk_e, n_used, w_up, w_down, layer)
    y2 = y.at[dest].get(mode="promise_in_bounds").reshape(B, So, TOP_K * D)
    return _combine(x_mid, mods, rt, y2, tile_off)


def _rope_swap_index():
    i = np.arange(MLA_ROPE)
    return (i // 16) * 16 + (1 - (i % 16) // 8) * 8 + i % 8


def _pack_w_in(w_in):
    sw = _rope_swap_index()
    kr = w_in[:, C_KR:C_NK]
    z64 = jnp.zeros((w_in.shape[0], 64), w_in.dtype)
    return jnp.concatenate([w_in[:, C_KV:C_KR], z64, kr, kr[:, sw], w_in[:, C_NK:]], axis=1).astype(BF16)


def _pack_w_uq(w_uq):
    sw = _rope_swap_index()
    w = w_uq.reshape(Q_LORA, HEADS, MLA_QK)
    return jnp.concatenate([w, w[:, :, MLA_NOPE:][:, :, sw]], axis=-1).reshape(Q_LORA, HEADS * HP).astype(BF16)


def _pack_w_ukv(w_ukv):
    w = w_ukv.reshape(KV_LORA, HEADS, MLA_NOPE + MLA_V)
    z = jnp.zeros((KV_LORA, HEADS, 64), w.dtype)
    kn = jnp.concatenate([w[:, :, :MLA_NOPE], z], axis=-1)
    v = w[:, :, MLA_NOPE:]
    even = (np.arange(HEADS) % 2 == 0)[None, :, None]
    vp = jnp.concatenate([jnp.where(even, v, 0.0), jnp.where(even, 0.0, v)], axis=-1)
    return jnp.concatenate([kn.reshape(KV_LORA, HEADS * HP), vp.reshape(KV_LORA, HEADS * HP)], axis=1).astype(BF16)


def _rope_lane_tables(L):
    t = np.arange(L)
    nf = MLA_ROPE // 4
    inv = ROPE_THETA ** (-np.arange(nf, dtype=np.float64) / nf)
    ang = np.concatenate([(t // GRID_W)[:, None] * inv, (t % GRID_W)[:, None] * inv], axis=-1)
    i = np.arange(MLA_ROPE)
    col = (i // 16) * nf + i % 8
    sgn = np.where((i % 16) // 8 == 0, -1.0, 1.0)
    cos = np.concatenate([np.ones((CTX, MLA_ROPE)), np.cos(ang)[:, col]], axis=0)
    sin = np.concatenate([np.zeros((CTX, MLA_ROPE)), np.sin(ang)[:, col] * sgn], axis=0)
    return cos.astype(np.float32), sin.astype(np.float32)


def _head_lane_weights(g, cos, sin, scale):
    sw = _rope_swap_index()
    S = cos.shape[0]
    g = g.astype(F32) * scale
    w1 = jnp.concatenate([jnp.broadcast_to(g[:MLA_NOPE], (S, MLA_NOPE)), g[MLA_NOPE:] * cos,
                          jnp.zeros((S, 32), F32)], axis=1)
    w2 = jnp.concatenate([jnp.zeros((S, MLA_NOPE), F32), g[MLA_NOPE:][sw] * sin, jnp.zeros((S, 32), F32)], axis=1)
    return w1, w2


def kernel(x, c, ctx, c_ctx, w_ada, b_ada, g_norm1, w_in, g_q_lat, w_uq, g_kv_lat, w_ukv, g_qn_mla, g_kn_mla,
           g_qn_nat, g_kn_nat, nat_rpb, s5_a_re, s5_a_im, s5_log_dt, s5_b_re, s5_b_im, s5_c_re, s5_c_im, s5_d,
           w_glu, w_branch, w_out, g_norm2, w_group, b_group, w_expert, b_expert, w_up, w_down):
    B, L, D = x.shape
    S = CTX + L
    assert ctx.shape[1] == CTX and D == D_MODEL and L % GRID_W == 0 and S % TM == 0
    depth = w_in.shape[0]
    rows = L // GRID_W
    cos, sin = _rope_lane_tables(L)
    rows_pad = -(-(B + 1) // 8) * 8
    c_all = jnp.concatenate([c, c_ctx[None], jnp.zeros((rows_pad - B - 1, D), c.dtype)], axis=0).astype(F32)

    xs = jnp.concatenate([ctx, x], axis=1).astype(F32)
    for l in range(depth):
        need_ctx = l < depth - 1
        ada = _ada(c_all, w_ada[l], b_ada[l])
        mods = jnp.stack([jnp.broadcast_to(ada[B].reshape(1, 6, D), (B, 6, D)), ada[:B].reshape(B, 6, D)], axis=1)

        kw1, kw2 = _head_lane_weights(g_kn_mla[l], cos, sin, 1.0)
        qw1, qw2 = _head_lane_weights(g_qn_mla[l], cos, sin, MLA_QK ** -0.5)
        gnq = jnp.tile(g_qn_nat[l].astype(F32) * NAT_DIM ** -0.5, 2).reshape(1, 128)
        gnk = jnp.tile(g_kn_nat[l].astype(F32), 2).reshape(1, 128)
        qm, km, vm, qn, kn, vn, u, gates = _front(
            xs, mods, g_norm1[l].reshape(1, D).astype(F32), _pack_w_in(w_in[l]),
            g_kv_lat[l].reshape(1, KV_LORA).astype(F32), _pack_w_ukv(w_ukv[l]),
            g_q_lat[l].reshape(1, Q_LORA).astype(F32), _pack_w_uq(w_uq[l]), kw1, kw2, qw1, qw2, gnq, gnk)

        oa = _mla(qm, km, vm, need_ctx)
        ob = _nat(qn, kn, vn, _nat_bias_table(nat_rpb[l], rows), need_ctx)
        ys = _s5(u, _s5_mats(s5_a_re[l], s5_a_im[l], s5_log_dt[l], s5_b_re[l], s5_b_im[l], s5_c_re[l], s5_c_im[l]),
                 s5_d[l])

        w_route = jnp.concatenate([w_group[l], w_expert[l], jnp.zeros((D, 128 - N_GROUPS - N_EXPERTS), F32)],
                                  axis=1).astype(F32)
        wrh, wrl = _split_bf16(w_route)
        b_route = jnp.concatenate([b_group[l], b_expert[l], jnp.zeros((128 - N_GROUPS - N_EXPERTS,), F32)]).reshape(1, 128)
        x_mid, h2, logits = _merge(
            xs, mods, oa, ob, ys, gates, w_glu[l].astype(BF16), w_branch[l].astype(BF16), w_out[l].astype(BF16), g_norm2[l].reshape(1, D).astype(F32), wrh, wrl,
            b_route.astype(F32))

        xs = _moe(x_mid, mods, h2, logits, w_up, w_down, l, 0 if need_ctx else CTX // RT)
    return xs.astype(x.dtype)
```

```python
import functools

import numpy as np
import jax
import jax.numpy as jnp
from jax import lax
from jax.experimental import pallas as pl
from jax.experimental.pallas import tpu as pltpu

F32 = jnp.float32
BF16 = jnp.bfloat16
HIGHEST = lax.Precision.HIGHEST

D_MODEL = 1024
GRID_W = 64
N_BRANCH = 3
BRANCH_W = 512
RMS_EPS = 1e-6
NEG = -1e30
HEADS = 8
MLA_NOPE = 64
MLA_ROPE = 32
MLA_QK = 96
MLA_V = 64
KV_LORA = 256
Q_LORA = 768
ROPE_THETA = 10000.0
NAT_DIM = 64
WIN_H = 8
WIN_W = 16
S5_W = 512
S5_GROUP = 16
S5_GROUPS = 32
S5_STATE = 64
N_GROUPS = 4
EXPERTS_PER_GROUP = 8
N_EXPERTS = 32
EXPERT_HIDDEN = 512
TOP_K = 2

CTX = 256
TM = 768
S5_CHUNK = 16
TC = TM // S5_CHUNK
RT = 256
MOE_BLOCK = 512
HP = 128
NAT_QROWS = 4
NAT_KROWS = NAT_QROWS + WIN_H
VMEM_LIMIT = 56 * 1024 * 1024
VMEM_LIMIT_FRONT = 60 * 1024 * 1024

P_KV = 0
P_KR = P_KV + KV_LORA
P_NK = P_KR + HP
P_NV = P_NK + 512
P_S5 = P_NV + 512
P_QL = P_S5 + 512
P_NQ = P_QL + Q_LORA
P_GATE = P_NQ + 512
P_END = P_GATE + N_BRANCH * D_MODEL

C_KV = 0
C_KR = C_KV + KV_LORA
C_NK = C_KR + MLA_ROPE
C_NV = C_NK + 512
C_S5 = C_NV + 512
C_QL = C_S5 + S5_W
C_NQ = C_QL + Q_LORA
C_GATE = C_NQ + 512


def _cparams(*sem, vmem=VMEM_LIMIT):
    return pltpu.CompilerParams(dimension_semantics=sem, vmem_limit_bytes=vmem)


def _const_spec(shape):
    nd = len(shape)
    return pl.BlockSpec(shape, lambda *_: (0,) * nd, pipeline_mode=pl.Buffered(1))


def _ada_kernel(c_ref, w_ref, b_ref, o_ref):
    c = c_ref[...]
    s = (c * jax.nn.sigmoid(c)).astype(BF16)
    o_ref[...] = jnp.dot(s, w_ref[...].astype(BF16), preferred_element_type=F32) + b_ref[...]


def _ada(c_all, w_ada, b_ada):
    rows, d = c_all.shape
    n = w_ada.shape[1]
    bn = 512
    return pl.pallas_call(
        _ada_kernel,
        out_shape=jax.ShapeDtypeStruct((rows, n), F32),
        grid=(n // bn,),
        in_specs=[pl.BlockSpec((rows, d), lambda i: (0, 0)),
                  pl.BlockSpec((d, bn), lambda i: (0, i)),
                  pl.BlockSpec((1, bn), lambda i: (0, i))],
        out_specs=pl.BlockSpec((rows, bn), lambda i: (0, i)),
        compiler_params=_cparams("arbitrary"),
        name="ada_mod",
    )(c_all, w_ada, b_ada.reshape(1, n))


def _tile_mod(mod_ref, idx, tile):
    ctx_v = mod_ref[0, 0, idx:idx + 1, :]
    lat_v = mod_ref[0, 1, idx:idx + 1, :]
    row = lax.broadcasted_iota(jnp.int32, (TM, 1), 0) + tile * TM
    return jnp.where(row < CTX, ctx_v, lat_v)


def _rms(x, g):
    return x * lax.rsqrt(jnp.mean(x * x, axis=-1, keepdims=True) + RMS_EPS) * g


def _mla_head(xh, w1, w2):
    lane = lax.broadcasted_iota(jnp.int32, xh.shape, 1)
    sq = jnp.where(lane < MLA_QK, xh * xh, 0.0)
    inv = lax.rsqrt(jnp.sum(sq, axis=-1, keepdims=True) * (1.0 / MLA_QK) + RMS_EPS)
    return inv * (xh * w1 + pltpu.roll(xh, HP - MLA_ROPE, 1) * w2)


def _pair_norm(x2, g2):
    lane = lax.broadcasted_iota(jnp.int32, x2.shape, 1)
    lo = lane < NAT_DIM
    sq = x2 * x2
    s_lo = jnp.sum(jnp.where(lo, sq, 0.0), axis=-1, keepdims=True)
    s_hi = jnp.sum(jnp.where(lo, 0.0, sq), axis=-1, keepdims=True)
    inv = lax.rsqrt(jnp.where(lo, s_lo, s_hi) * (1.0 / NAT_DIM) + RMS_EPS)
    return x2 * inv * g2


def _lane_group(shape):
    return lax.broadcasted_iota(jnp.int32, shape, 1) // S5_GROUP


def _chunk_shuffle(u_scr, ug_ref):
    grp = _lane_group((TC, 128))
    for q in range(S5_W // 128):
        for hf in range(2):
            tiles = [u_scr[q, pl.ds(8 * hf + sp, TC, stride=S5_CHUNK), :] for sp in range(8)]
            for gg in range(8):
                acc = None
                for sp in range(8):
                    sh = ((sp - gg) * S5_GROUP) % 128
                    r = tiles[sp] if sh == 0 else pltpu.roll(tiles[sp], sh, 1)
                    acc = r if acc is None else jnp.where(grp == sp, r, acc)
                ug_ref[0, 8 * q + gg, :, hf * 128:(hf + 1) * 128] = acc.astype(BF16)


def _chunk_unshuffle(yg_ref, y_scr):
    grp = _lane_group((TC, 128))
    for q in range(S5_W // 128):
        for hf in range(2):
            tiles = [yg_ref[0, 8 * q + gg, :, hf * 128:(hf + 1) * 128].astype(F32) for gg in range(8)]
            for tp in range(8):
                acc = None
                for gg in range(8):
                    sh = ((gg - tp) * S5_GROUP) % 128
                    r = tiles[gg] if sh == 0 else pltpu.roll(tiles[gg], sh, 1)
                    acc = r if acc is None else jnp.where(grp == gg, r, acc)
                y_scr[q, pl.ds(8 * hf + tp, TC, stride=S5_CHUNK), :] = acc


def _front_kernel(x_ref, mod_ref, g1_ref, win_ref, gkv_ref, wukv_ref, gq_ref, wuq_ref,
                  kw1_ref, kw2_ref, qw1_ref, qw2_ref, gnq_ref, gnk_ref,
                  qm_ref, km_ref, vm_ref, qn_ref, kn_ref, vn_ref, ug_ref, gate_ref, u_scr):
    j = pl.program_id(1)
    sh = _tile_mod(mod_ref, 0, j)
    sc = _tile_mod(mod_ref, 1, j)
    h = (_rms(x_ref[0], g1_ref[...]) * (1.0 + sc) + sh).astype(BF16)

    def proj(a, b):
        return jnp.dot(h, win_ref[:, a:b], preferred_element_type=F32)

    kvl = _rms(proj(P_KV, P_KR), gkv_ref[...]).astype(BF16)
    krp = proj(P_KR, P_NK)
    kw1 = kw1_ref[...]
    kw2 = kw2_ref[...]
    for hd in range(HEADS):
        kn = jnp.dot(kvl, wukv_ref[:, hd * HP:(hd + 1) * HP], preferred_element_type=F32)
        km_ref[0, :, hd * HP:(hd + 1) * HP] = _mla_head(kn + krp, kw1, kw2).astype(BF16)
    vm_ref[0] = jnp.dot(kvl, wukv_ref[:, HEADS * HP:], preferred_element_type=F32).astype(BF16)

    ql = _rms(proj(P_QL, P_NQ), gq_ref[...]).astype(BF16)
    qw1 = qw1_ref[...]
    qw2 = qw2_ref[...]
    for hd in range(HEADS):
        qh = jnp.dot(ql, wuq_ref[:, hd * HP:(hd + 1) * HP], preferred_element_type=F32)
        qm_ref[0, :, hd * HP:(hd + 1) * HP] = _mla_head(qh, qw1, qw2).astype(BF16)

    gnq = gnq_ref[...]
    gnk = gnk_ref[...]
    for c in range(4):
        kk = proj(P_NK + c * 128, P_NK + (c + 1) * 128)
        kn_ref[0, :, c * 128:(c + 1) * 128] = _pair_norm(kk, gnk).astype(BF16)
        qq = proj(P_NQ + c * 128, P_NQ + (c + 1) * 128)
        qn_ref[0, :, c * 128:(c + 1) * 128] = _pair_norm(qq, gnq).astype(BF16)
    vn_ref[0] = proj(P_NV, P_S5).astype(BF16)

    for q in range(S5_W // 128):
        u_scr[q] = proj(P_S5 + q * 128, P_S5 + (q + 1) * 128)
    _chunk_shuffle(u_scr, ug_ref)
    for c in range(N_BRANCH * D_MODEL // 512):
        gate_ref[0, :, c * 512:(c + 1) * 512] = jax.nn.sigmoid(
            proj(P_GATE + c * 512, P_GATE + (c + 1) * 512)).astype(BF16)


def _front(xs, mods, g1, w_in_p, gkv, wukv_p, gq, wuq_p, kw1, kw2, qw1, qw2, gnq, gnk):
    B, S, D = xs.shape
    nt = S // TM
    tok = lambda w: pl.BlockSpec((1, TM, w), lambda b, j: (b, j, 0))
    tab = pl.BlockSpec((TM, HP), lambda b, j: (j, 0))
    outs = [HEADS * HP, HEADS * HP, HEADS * HP, 512, 512, 512, None, N_BRANCH * D_MODEL]
    cw = S5_CHUNK * S5_GROUP
    ug_shape = jax.ShapeDtypeStruct((B, S5_GROUPS, S // S5_CHUNK, cw), BF16)
    ug_spec = pl.BlockSpec((1, S5_GROUPS, TC, cw), lambda b, j: (b, 0, j, 0))
    return pl.pallas_call(
        _front_kernel,
        out_shape=[ug_shape if w is None else jax.ShapeDtypeStruct((B, S, w), BF16) for w in outs],
        grid=(B, nt),
        in_specs=[tok(D),
                  pl.BlockSpec((1, 2, 6, D), lambda b, j: (b, 0, 0, 0)),
                  _const_spec((1, D)), _const_spec(w_in_p.shape), _const_spec((1, KV_LORA)),
                  _const_spec(wukv_p.shape), _const_spec((1, Q_LORA)), _const_spec(wuq_p.shape),
                  tab, tab, tab, tab, _const_spec((1, 128)), _const_spec((1, 128))],
        out_specs=[ug_spec if w is None else tok(w) for w in outs],
        scratch_shapes=[pltpu.VMEM((S5_W // 128, TM, 128), F32)],
        compiler_params=_cparams("parallel", "arbitrary", vmem=VMEM_LIMIT_FRONT),
        name="mixer_front",
    )(xs, mods, g1, w_in_p, gkv, wukv_p, gq, wuq_p, kw1, kw2, qw1, qw2, gnq, gnk)


def _nt(a, b):
    return lax.dot_general(a, b, (((1,), (1,)), ((), ())), preferred_element_type=F32)


def _softmax_pv(s, v):
    m = jnp.max(s, axis=-1, keepdims=True)
    p = jnp.exp(s - m)
    l = jnp.sum(p, axis=-1, keepdims=True)
    return jnp.dot(p.astype(BF16), v, preferred_element_type=F32) * (1.0 / l)


def _mla_kernel(q_ref, k_ref, v_ref, o_ref, *, seq, need_ctx):
    j = pl.program_id(2)

    def attend(r0, nr, nk):
        o = None
        for hh in range(2):
            sl = slice(hh * HP, (hh + 1) * HP)
            s = _nt(q_ref[0, r0:r0 + nr, sl], k_ref[0, :nk, sl])
            t = _softmax_pv(s, v_ref[0, :nk, sl])
            o = t if o is None else o + t
        o_ref[0, r0:r0 + nr, :] = o.astype(BF16)

    @pl.when(j == 0)
    def _():
        if need_ctx:
            attend(0, CTX, CTX)
        else:
            o_ref[0, 0:CTX, :] = jnp.zeros((CTX, 128), BF16)
        attend(CTX, TM - CTX, seq)

    @pl.when(j > 0)
    def _():
        attend(0, TM, seq)


def _mla(qm, km, vm, need_ctx):
    B, S, _ = qm.shape
    nt = S // TM
    return pl.pallas_call(
        functools.partial(_mla_kernel, seq=S, need_ctx=need_ctx),
        out_shape=jax.ShapeDtypeStruct((B, S, BRANCH_W), BF16),
        grid=(B, HEADS // 2, nt),
        in_specs=[pl.BlockSpec((1, TM, 2 * HP), lambda b, h, j: (b, j, h)),
                  pl.BlockSpec((1, S, 2 * HP), lambda b, h, j: (b, 0, h)),
                  pl.BlockSpec((1, S, 2 * HP), lambda b, h, j: (b, 0, h))],
        out_specs=pl.BlockSpec((1, TM, 128), lambda b, h, j: (b, j, h)),
        compiler_params=_cparams("parallel", "parallel", "arbitrary"),
        name="mla_attn",
    )(qm, km, vm)


def _nat_kernel(q_ref, k_ref, v_ref, bias_ref, o_ref, *, rows, need_ctx):
    lane = lax.broadcasted_iota(jnp.int32, (1, 128), 1)
    lo = lane < NAT_DIM
    masks = (lo, jnp.logical_not(lo))
    nq = NAT_QROWS * GRID_W
    nk = NAT_KROWS * GRID_W
    n_blocks = rows // NAT_QROWS

    def half(x, hh):
        return jnp.where(masks[hh], x, jnp.zeros_like(x))

    kc = k_ref[0, 0:CTX, :]
    vc = v_ref[0, 0:CTX, :]

    if need_ctx:
        qc = q_ref[0, 0:CTX, :]
        o = None
        for hh in range(2):
            t = _softmax_pv(_nt(half(qc, hh), kc), half(vc, hh))
            o = t if o is None else o + t
        o_ref[0, 0:CTX, :] = o.astype(BF16)
    else:
        o_ref[0, 0:CTX, :] = jnp.zeros((CTX, 128), BF16)

    def block(i, carry):
        ws = jnp.clip(i * NAT_QROWS - WIN_H // 2, 0, rows - NAT_KROWS)
        cls = jnp.where(i == 0, 0, jnp.where(i == n_blocks - 1, 2, 1))
        q0 = pl.multiple_of(CTX + i * nq, nq)
        k0 = pl.multiple_of(CTX + ws * GRID_W, GRID_W)
        q = q_ref[0, pl.ds(q0, nq), :]
        kw = k_ref[0, pl.ds(k0, nk), :]
        vw = v_ref[0, pl.ds(k0, nk), :]
        o = None
        for hh in range(2):
            qh = half(q, hh)
            s_lat = _nt(qh, kw) + bias_ref[0, hh, cls]
            s_ctx = _nt(qh, kc)
            m = jnp.maximum(jnp.max(s_lat, axis=-1, keepdims=True), jnp.max(s_ctx, axis=-1, keepdims=True))
            p_lat = jnp.exp(s_lat - m)
            p_ctx = jnp.exp(s_ctx - m)
            l = jnp.sum(p_lat, axis=-1, keepdims=True) + jnp.sum(p_ctx, axis=-1, keepdims=True)
            acc = (jnp.dot(p_lat.astype(BF16), half(vw, hh), preferred_element_type=F32)
                   + jnp.dot(p_ctx.astype(BF16), half(vc, hh), preferred_element_type=F32))
            t = acc * (1.0 / l)
            o = t if o is None else o + t
        o_ref[0, pl.ds(q0, nq), :] = o.astype(BF16)
        return carry

    lax.fori_loop(0, n_blocks, block, 0)


def _nat(qn, kn, vn, bias, need_ctx):
    B, S, _ = qn.shape
    rows = (S - CTX) // GRID_W
    seq = pl.BlockSpec((1, S, 128), lambda h, b: (b, 0, h))
    return pl.pallas_call(
        functools.partial(_nat_kernel, rows=rows, need_ctx=need_ctx),
        out_shape=jax.ShapeDtypeStruct((B, S, BRANCH_W), BF16),
        grid=(HEADS // 2, B),
        in_specs=[seq, seq, seq,
                  pl.BlockSpec((1, 2, 3, NAT_QROWS * GRID_W, NAT_KROWS * GRID_W), lambda h, b: (h, 0, 0, 0, 0))],
        out_specs=seq,
        compiler_params=_cparams("parallel", "arbitrary"),
        name="nat_attn",
    )(qn, kn, vn, bias)


def _nat_bias_table(rpb, rows):
    assert rows % NAT_QROWS == 0 and rows >= NAT_KROWS
    n_blocks = rows // NAT_QROWS
    cols = np.arange(GRID_W)
    cs = np.clip(cols - WIN_W // 2, 0, GRID_W - WIN_W)
    kcol = cols[None, :]
    qcol = cols[:, None]
    vcol = (kcol >= cs[:, None]) & (kcol < cs[:, None] + WIN_W)
    dcol = np.clip(kcol - qcol + WIN_W - 1, 0, 2 * WIN_W - 2)
    blk = np.array([0, 1, n_blocks - 1])
    ws = np.clip(blk * NAT_QROWS - WIN_H // 2, 0, rows - NAT_KROWS)
    r = blk[:, None] * NAT_QROWS + np.arange(NAT_QROWS)[None, :]
    rs = np.clip(r - WIN_H // 2, 0, rows - WIN_H)
    krow = ws[:, None, None] + np.arange(NAT_KROWS)[None, None, :]
    vrow = (krow >= rs[:, :, None]) & (krow < rs[:, :, None] + WIN_H)
    drow = np.clip(krow - r[:, :, None] + WIN_H - 1, 0, 2 * WIN_H - 2)
    t = rpb.astype(F32)[:, drow]
    t = t[..., dcol]
    valid = vrow[:, :, :, None, None] & vcol[None, None, None]
    t = jnp.where(valid[None], t, NEG)
    t = t.transpose(0, 1, 2, 4, 3, 5).reshape(HEADS, 3, NAT_QROWS * GRID_W, NAT_KROWS * GRID_W)
    return t.reshape(HEADS // 2, 2, 3, NAT_QROWS * GRID_W, NAT_KROWS * GRID_W)


def _s5_kernel(u_ref, mt_ref, f_ref, e_ref, a_ref, d_ref, y_ref, s_scr, hin_scr, *, batch, n_ctx, n_all):
    u = u_ref[0]
    for d in range(2):
        s_scr[d] = jnp.dot(u, f_ref[d, 0], preferred_element_type=F32)

    def make_step(d):
        a1 = a_ref[d, 0, 0:1, :]
        a2 = a_ref[d, 0, 1:2, :]
        a3 = a_ref[d, 0, 2:3, :]

        def step(c, hh):
            h, hs = hh
            r0 = pl.multiple_of(c * batch, batch)
            hin_scr[d, pl.ds(r0, batch), :] = h
            s = s_scr[d, pl.ds(r0, batch), :]
            return (h * a1 + hs * a2 + s[:, :128], hs * a1 + h * a3 + s[:, 128:])
        return step

    z = jnp.zeros((batch, 128), F32)
    lax.fori_loop(0, n_all, make_step(0), (z, z))
    rstep = make_step(1)
    hh = lax.fori_loop(0, n_ctx, lambda i, c: rstep(n_ctx - 1 - i, c), (z, z))
    lax.fori_loop(0, n_all - n_ctx, lambda i, c: rstep(n_all - 1 - i, c), hh)

    y = jnp.dot(u, mt_ref[0], preferred_element_type=F32) + u.astype(F32) * d_ref[0]
    for d in range(2):
        y = y + jnp.dot(hin_scr[d].astype(BF16), e_ref[d, 0], preferred_element_type=F32)
    y_ref[0] = y.astype(BF16)


def _s5(ug, mats, d_skip):
    mt, fcat, e, a = mats
    B, _, C, w = ug.shape
    ug = ug.transpose(1, 2, 0, 3).reshape(S5_GROUPS, C * B, w)
    dg = jnp.tile(d_skip.astype(F32).reshape(S5_GROUPS, 1, S5_GROUP), (1, 1, S5_CHUNK))
    yg = pl.pallas_call(
        functools.partial(_s5_kernel, batch=B, n_ctx=CTX // S5_CHUNK, n_all=C),
        out_shape=jax.ShapeDtypeStruct((S5_GROUPS, C * B, w), BF16),
        grid=(S5_GROUPS,),
        in_specs=[pl.BlockSpec((1, C * B, w), lambda g: (g, 0, 0)),
                  pl.BlockSpec((1, w, w), lambda g: (g, 0, 0)),
                  pl.BlockSpec((2, 1, w, 256), lambda g: (0, g, 0, 0)),
                  pl.BlockSpec((2, 1, 128, w), lambda g: (0, g, 0, 0)),
                  pl.BlockSpec((2, 1, 3, 128), lambda g: (0, g, 0, 0)),
                  pl.BlockSpec((1, 1, w), lambda g: (g, 0, 0))],
        out_specs=pl.BlockSpec((1, C * B, w), lambda g: (g, 0, 0)),
        scratch_shapes=[pltpu.VMEM((2, C * B, 256), F32), pltpu.VMEM((2, C * B, 128), F32)],
        compiler_params=_cparams("parallel"),
        name="s5_scan",
    )(ug, mt, fcat, e, a, dg)
    return yg.reshape(S5_GROUPS, C, B, w).transpose(2, 0, 1, 3)


def _s5_mats(a_re, a_im, log_dt, b_re, b_im, c_re, c_im):
    L = S5_CHUNK
    lam_re = jnp.minimum(a_re.astype(F32), -1e-4)
    lam_im = a_im.astype(F32)
    dt = jnp.exp(log_dt.astype(F32))[..., None]
    mag = jnp.exp(lam_re * dt)
    ab_re = mag * jnp.cos(lam_im * dt)
    ab_im = mag * jnp.sin(lam_im * dt)
    den = lam_re * lam_re + lam_im * lam_im
    n_re = ab_re - 1.0
    f_re = (n_re * lam_re + ab_im * lam_im) / den
    f_im = (ab_im * lam_re - n_re * lam_im) / den
    br = b_re.astype(F32)
    bi = b_im.astype(F32)
    bb_re = f_re[..., None] * br - f_im[..., None] * bi
    bb_im = f_re[..., None] * bi + f_im[..., None] * br
    tau = jnp.arange(L + 1, dtype=F32)
    pm = jnp.exp((lam_re * dt)[..., None] * tau)
    ph = (lam_im * dt)[..., None] * tau
    pw_re = pm * jnp.cos(ph)
    pw_im = pm * jnp.sin(ph)
    cr = c_re.astype(F32)
    ci = c_im.astype(F32)
    pwt_re = pw_re.transpose(0, 1, 3, 2)[:, :, :, None, :]
    pwt_im = pw_im.transpose(0, 1, 3, 2)[:, :, :, None, :]
    ca_re = cr[:, :, None] * pwt_re - ci[:, :, None] * pwt_im
    ca_im = cr[:, :, None] * pwt_im + ci[:, :, None] * pwt_re
    kk = (jnp.einsum('dgtnp,dgpm->dgtnm', ca_re, bb_re, precision=HIGHEST)
          - jnp.einsum('dgtnp,dgpm->dgtnm', ca_im, bb_im, precision=HIGHEST))
    t = np.arange(L)
    lag_f = t[:, None] - t[None, :]
    m_f = jnp.where((lag_f >= 0)[None, :, :, None, None], kk[0][:, np.clip(lag_f, 0, L)], 0.0)
    m_r = jnp.where((lag_f <= 0)[None, :, :, None, None], kk[1][:, np.clip(-lag_f, 0, L)], 0.0)
    mt = (m_f + m_r).transpose(0, 2, 4, 1, 3).reshape(S5_GROUPS, L * S5_GROUP, L * S5_GROUP)
    pow_f = np.stack([L - 1 - t, t])
    pr = jnp.stack([pw_re[d][:, :, pow_f[d]] for d in range(2)])
    pi = jnp.stack([pw_im[d][:, :, pow_f[d]] for d in range(2)])
    fr = pr[..., None] * bb_re[:, :, :, None, :] - pi[..., None] * bb_im[:, :, :, None, :]
    fi = pr[..., None] * bb_im[:, :, :, None, :] + pi[..., None] * bb_re[:, :, :, None, :]
    fr = fr.transpose(0, 1, 3, 4, 2).reshape(2, S5_GROUPS, L * S5_GROUP, S5_STATE)
    fi = fi.transpose(0, 1, 3, 4, 2).reshape(2, S5_GROUPS, L * S5_GROUP, S5_STATE)
    fcat = jnp.concatenate([fr, fi, fi, fr], axis=-1)
    pow_e = np.stack([t + 1, L - t])
    er = jnp.stack([ca_re[d][:, pow_e[d]] for d in range(2)])
    ei = jnp.stack([ca_im[d][:, pow_e[d]] for d in range(2)])
    er = er.transpose(0, 1, 4, 2, 3).reshape(2, S5_GROUPS, S5_STATE, L * S5_GROUP)
    ei = ei.transpose(0, 1, 4, 2, 3).reshape(2, S5_GROUPS, S5_STATE, L * S5_GROUP)
    e = jnp.concatenate([er, -ei], axis=2)
    ar = pw_re[..., L]
    ai = pw_im[..., L]
    a = jnp.stack([jnp.concatenate([ar, ar], -1), jnp.concatenate([-ai, ai], -1),
                   jnp.concatenate([ai, -ai], -1)], axis=2)
    return mt.astype(BF16), fcat.astype(BF16), e.astype(BF16), a


def _split_bf16(v):
    hi = v.astype(BF16)
    return hi, (v - hi.astype(F32)).astype(BF16)


def _merge_kernel(x_ref, mod_ref, oa_ref, ob_ref, yg_ref, gate_ref, wglu_ref, wbr_ref,
                  wout_ref, g2_ref, wrh_ref, wrl_ref, br_ref, xo_ref, h2_ref, lg_ref, y_scr):
    j = pl.program_id(1)
    _chunk_unshuffle(yg_ref, y_scr)
    yc = jnp.concatenate([y_scr[q] for q in range(S5_W // 128)], axis=1)
    z = jnp.dot(jax.nn.gelu(yc).astype(BF16), wglu_ref[...], preferred_element_type=F32)
    oc = (z[:, :S5_W] * jax.nn.sigmoid(z[:, S5_W:])).astype(BF16)
    outs = (oa_ref[0], ob_ref[0], oc)
    y = None
    for i in range(N_BRANCH):
        t = gate_ref[0, :, i * D_MODEL:(i + 1) * D_MODEL].astype(F32) * jnp.dot(
            outs[i], wbr_ref[i], preferred_element_type=F32)
        y = t if y is None else y + t
    y2 = jnp.dot(y.astype(BF16), wout_ref[...], preferred_element_type=F32)
    xn = x_ref[0] + _tile_mod(mod_ref, 2, j) * y2
    xo_ref[0] = xn
    h2 = _rms(xn, g2_ref[...]) * (1.0 + _tile_mod(mod_ref, 4, j)) + _tile_mod(mod_ref, 3, j)
    hi, lo = _split_bf16(h2)
    h2_ref[0] = hi
    lg_ref[0] = (jnp.dot(hi, wrh_ref[...], preferred_element_type=F32)
                 + jnp.dot(lo, wrh_ref[...], preferred_element_type=F32)
                 + jnp.dot(hi, wrl_ref[...], preferred_element_type=F32) + br_ref[...])


def _merge(xs, mods, oa, ob, yg, gates, wglu, wbr, wout, g2, wrh, wrl, br):
    B, S, D = xs.shape
    nt = S // TM
    tok = lambda w: pl.BlockSpec((1, TM, w), lambda b, j: (b, j, 0))
    return pl.pallas_call(
        _merge_kernel,
        out_shape=[jax.ShapeDtypeStruct((B, S, D), F32), jax.ShapeDtypeStruct((B, S, D), BF16),
                   jax.ShapeDtypeStruct((B, S, 128), F32)],
        grid=(B, nt),
        in_specs=[tok(D),
                  pl.BlockSpec((1, 2, 6, D), lambda b, j: (b, 0, 0, 0)),
                  tok(BRANCH_W), tok(BRANCH_W),
                  pl.BlockSpec((1, S5_GROUPS, TC, S5_CHUNK * S5_GROUP), lambda b, j: (b, 0, j, 0)),
                  tok(N_BRANCH * D_MODEL), _const_spec(wglu.shape), _const_spec(wbr.shape),
                  _const_spec(wout.shape), _const_spec((1, D)), _const_spec(wrh.shape),
                  _const_spec(wrl.shape), _const_spec((1, 128))],
        out_specs=[tok(D), tok(D), tok(128)],
        scratch_shapes=[pltpu.VMEM((S5_W // 128, TM, 128), F32)],
        compiler_params=_cparams("parallel", "arbitrary"),
        name="merge_router",
    )(xs, mods, oa, ob, yg, gates, wglu, wbr, wout, g2, wrh, wrl, br)


def _expert_kernel(be_ref, nu_ref, x_ref, wup_ref, wdn_ref, y_ref, wup_scr, wdn_scr):
    i = pl.program_id(0)
    prev = be_ref[jnp.maximum(i - 1, 0)]

    @pl.when((i == 0) | (be_ref[i] != prev))
    def _():
        wup_scr[...] = wup_ref[0, 0].astype(BF16)
        wdn_scr[...] = wdn_ref[0, 0].astype(BF16)

    @pl.when(i < nu_ref[0])
    def _():
        a = jnp.dot(x_ref[...], wup_scr[...], preferred_element_type=F32)
        g = a[:, :EXPERT_HIDDEN]
        hid = (g * jax.nn.sigmoid(g) * a[:, EXPERT_HIDDEN:]).astype(BF16)
        y_ref[...] = jnp.dot(hid, wdn_scr[...], preferred_element_type=F32).astype(BF16)

    @pl.when(i >= nu_ref[0])
    def _():
        y_ref[...] = jnp.zeros(y_ref.shape, BF16)


def _experts(xg, blk_e, n_used, w_up, w_down, layer):
    n_rows, D = xg.shape
    n_blk = n_rows // MOE_BLOCK
    gs = pltpu.PrefetchScalarGridSpec(
        num_scalar_prefetch=2,
        grid=(n_blk,),
        in_specs=[pl.BlockSpec((MOE_BLOCK, D), lambda i, be, nu: (i, 0)),
                  pl.BlockSpec((1, 1, D, 2 * EXPERT_HIDDEN), lambda i, be, nu: (layer, be[i], 0, 0)),
                  pl.BlockSpec((1, 1, EXPERT_HIDDEN, D), lambda i, be, nu: (layer, be[i], 0, 0))],
        out_specs=pl.BlockSpec((MOE_BLOCK, D), lambda i, be, nu: (i, 0)),
        scratch_shapes=[pltpu.VMEM((D, 2 * EXPERT_HIDDEN), BF16), pltpu.VMEM((EXPERT_HIDDEN, D), BF16)],
    )
    return pl.pallas_call(
        _expert_kernel,
        out_shape=jax.ShapeDtypeStruct((n_rows, D), BF16),
        grid_spec=gs,
        compiler_params=_cparams("arbitrary"),
        name="moe_experts",
    )(blk_e, n_used, xg, w_up, w_down)


R_E0, R_E1, R_RANK0, R_RANK1, R_G0, R_G1 = 0, 1, 2, 3, 4, 5


def _router_kernel(lg_ref, tri_ref, rt_ref, cnt_ref, base_scr):
    first = (pl.program_id(0) == 0) & (pl.program_id(1) == 0)

    @pl.when(first)
    def _():
        base_scr[...] = jnp.zeros(base_scr.shape, F32)

    lg = lg_ref[0]
    lane = lax.broadcasted_iota(jnp.int32, lg.shape, 1)
    big = jnp.int32(128)

    def first_max(v):
        m = jnp.max(v, axis=-1, keepdims=True)
        return m, jnp.min(jnp.where(v == m, lane, big), axis=-1, keepdims=True)

    gmask = lane < N_GROUPS
    gmax, grp = first_max(jnp.where(gmask, lg, -jnp.inf))
    p_grp = 1.0 / jnp.sum(jnp.where(gmask, jnp.exp(lg - gmax), 0.0), axis=-1, keepdims=True)
    e_lo = N_GROUPS + grp * EXPERTS_PER_GROUP
    el = jnp.where((lane >= e_lo) & (lane < e_lo + EXPERTS_PER_GROUP), lg, -jnp.inf)
    v1, i1 = first_max(el)
    v2, i2 = first_max(jnp.where(lane == i1, -jnp.inf, el))
    e2 = jnp.exp(v2 - v1)
    g1 = p_grp / (1.0 + e2)
    g2 = g1 * e2
    hot1 = lane == i1
    hot2 = lane == i2
    onehot = jnp.where(hot1 | hot2, 1.0, 0.0)
    cnt = jnp.dot(tri_ref[...], onehot.astype(BF16), preferred_element_type=F32) + base_scr[...]
    r1 = jnp.sum(jnp.where(hot1, cnt, 0.0), axis=-1, keepdims=True)
    r2 = jnp.sum(jnp.where(hot2, cnt, 0.0), axis=-1, keepdims=True)
    base_scr[...] = base_scr[...] + jnp.sum(onehot, axis=0, keepdims=True)
    cnt_ref[...] = base_scr[...]
    out = jnp.zeros(lg.shape, F32)
    for k, v in ((R_E0, (i1 - N_GROUPS).astype(F32)), (R_E1, (i2 - N_GROUPS).astype(F32)),
                 (R_RANK0, r1), (R_RANK1, r2), (R_G0, g1), (R_G1, g2)):
        out = jnp.where(lane == k, v, out)
    rt_ref[0] = out


def _router(logits, tile_off):
    B, S, _ = logits.shape
    nt = S // RT - tile_off
    tri = (np.arange(RT)[None, :] < np.arange(RT)[:, None]).astype(np.float32)
    return pl.pallas_call(
        _router_kernel,
        out_shape=[jax.ShapeDtypeStruct((B, nt * RT, 128), F32), jax.ShapeDtypeStruct((1, 128), F32)],
        grid=(B, nt),
        in_specs=[pl.BlockSpec((1, RT, 128), lambda b, j: (b, j + tile_off, 0)), _const_spec((RT, RT))],
        out_specs=[pl.BlockSpec((1, RT, 128), lambda b, j: (b, j, 0)), pl.BlockSpec((1, 128), lambda b, j: (0, 0))],
        scratch_shapes=[pltpu.VMEM((1, 128), F32)],
        compiler_params=_cparams("arbitrary", "arbitrary"),
        name="moe_router",
    )(logits, jnp.asarray(tri, BF16))


def _combine_kernel(x_ref, mod_ref, rt_ref, y0_ref, y1_ref, o_ref):
    rt = rt_ref[0]
    y0 = y0_ref[0, 0].astype(F32)
    y1 = y1_ref[0, 0].astype(F32)
    o_ref[0] = x_ref[0] + mod_ref[0, 0, 5:6, :] * (rt[:, R_G0:R_G0 + 1] * y0 + rt[:, R_G1:R_G1 + 1] * y1)


def _combine(x_mid, mods, rt, y2, tile_off):
    B, S, D = x_mid.shape
    So = rt.shape[1]
    tok = lambda w: pl.BlockSpec((1, RT, w), lambda b, j: (b, j, 0))
    return pl.pallas_call(
        _combine_kernel,
        out_shape=jax.ShapeDtypeStruct((B, So, D), F32),
        grid=(B, So // RT),
        in_specs=[pl.BlockSpec((1, RT, D), lambda b, j: (b, j + tile_off, 0)),
                  pl.BlockSpec((1, 1, 6, D), lambda b, j: (b, jnp.minimum(j + tile_off, 1), 0, 0)),
                  tok(128), pl.BlockSpec((1, 1, RT, D), lambda b, j: (0, b, j, 0)),
                  pl.BlockSpec((1, 1, RT, D), lambda b, j: (1, b, j, 0))],
        out_specs=tok(D),
        compiler_params=_cparams("parallel", "arbitrary"),
        name="moe_combine",
    )(x_mid, mods, rt, y2, y2)


def _moe(x_mid, mods, h2, logits, w_up, w_down, layer, tile_off):
    B, S, D = x_mid.shape
    rt, cnt = _router(logits, tile_off)
    So = rt.shape[1]
    T = B * So
    S2 = T * TOP_K
    counts = cnt[0, N_GROUPS:N_GROUPS + N_EXPERTS].astype(jnp.int32)
    padded = (counts + MOE_BLOCK - 1) // MOE_BLOCK * MOE_BLOCK
    p_end = jnp.cumsum(padded)
    p_start = p_end - padded
    rtf = rt.reshape(T, 128)
    eid = rtf[:, R_E0:R_E1 + 1].astype(jnp.int32)
    rank = rtf[:, R_RANK0:R_RANK1 + 1].astype(jnp.int32)
    dest = (jnp.sum(jnp.where(eid[:, :, None] == jnp.arange(N_EXPERTS, dtype=jnp.int32), p_start, 0), axis=-1)
            + rank).reshape(S2)
    n_blk = -(-(S2 + N_EXPERTS * (MOE_BLOCK - 1)) // MOE_BLOCK)
    rows = (jnp.arange(B, dtype=jnp.int32)[:, None] * S + tile_off * RT
            + jnp.arange(So, dtype=jnp.int32)[None, :]).reshape(T)
    buf_row = jnp.zeros((n_blk * MOE_BLOCK,), jnp.int32).at[dest].set(
        jnp.repeat(rows, TOP_K), mode="promise_in_bounds", unique_indices=True)
    blk_start = jnp.arange(n_blk, dtype=jnp.int32) * MOE_BLOCK
    blk_e = jnp.minimum(jnp.sum((p_end[None, :] <= blk_start[:, None]).astype(jnp.int32), axis=1),
                        N_EXPERTS - 1).astype(jnp.int32)
    n_used = (p_end[-1] // MOE_BLOCK).astype(jnp.int32).reshape(1)
    xg = h2.reshape(B * S, D).at[buf_row].get(mode="promise_in_bounds")
    y = _experts(xg, blk_e, n_used, w_up, w_down, layer)
    y2 = y.at[dest.reshape(T, TOP_K).T.reshape(S2)].get(mode="promise_in_bounds").reshape(TOP_K, B, So, D)
    return _combine(x_mid, mods, rt, y2, tile_off)


def _rope_swap_index():
    i = np.arange(MLA_ROPE)
    return (i // 16) * 16 + (1 - (i % 16) // 8) * 8 + i % 8


def _pack_w_in(w_in):
    sw = _rope_swap_index()
    kr = w_in[:, C_KR:C_NK]
    z64 = jnp.zeros((w_in.shape[0], 64), w_in.dtype)
    return jnp.concatenate([w_in[:, C_KV:C_KR], z64, kr, kr[:, sw], w_in[:, C_NK:]], axis=1).astype(BF16)


def _pack_w_uq(w_uq):
    sw = _rope_swap_index()
    w = w_uq.reshape(Q_LORA, HEADS, MLA_QK)
    return jnp.concatenate([w, w[:, :, MLA_NOPE:][:, :, sw]], axis=-1).reshape(Q_LORA, HEADS * HP).astype(BF16)


def _pack_w_ukv(w_ukv):
    w = w_ukv.reshape(KV_LORA, HEADS, MLA_NOPE + MLA_V)
    z = jnp.zeros((KV_LORA, HEADS, 64), w.dtype)
    kn = jnp.concatenate([w[:, :, :MLA_NOPE], z], axis=-1)
    v = w[:, :, MLA_NOPE:]
    even = (np.arange(HEADS) % 2 == 0)[None, :, None]
    vp = jnp.concatenate([jnp.where(even, v, 0.0), jnp.where(even, 0.0, v)], axis=-1)
    return jnp.concatenate([kn.reshape(KV_LORA, HEADS * HP), vp.reshape(KV_LORA, HEADS * HP)], axis=1).astype(BF16)


def _rope_lane_tables(L):
    t = np.arange(L)
    nf = MLA_ROPE // 4
    inv = ROPE_THETA ** (-np.arange(nf, dtype=np.float64) / nf)
    ang = np.concatenate([(t // GRID_W)[:, None] * inv, (t % GRID_W)[:, None] * inv], axis=-1)
    i = np.arange(MLA_ROPE)
    col = (i // 16) * nf + i % 8
    sgn = np.where((i % 16) // 8 == 0, -1.0, 1.0)
    cos = np.concatenate([np.ones((CTX, MLA_ROPE)), np.cos(ang)[:, col]], axis=0)
    sin = np.concatenate([np.zeros((CTX, MLA_ROPE)), np.sin(ang)[:, col] * sgn], axis=0)
    return cos.astype(np.float32), sin.astype(np.float32)


def _head_lane_weights(g, cos, sin, scale):
    sw = _rope_swap_index()
    S = cos.shape[0]
    g = g.astype(F32) * scale
    w1 = jnp.concatenate([jnp.broadcast_to(g[:MLA_NOPE], (S, MLA_NOPE)), g[MLA_NOPE:] * cos,
                          jnp.zeros((S, 32), F32)], axis=1)
    w2 = jnp.concatenate([jnp.zeros((S, MLA_NOPE), F32), g[MLA_NOPE:][sw] * sin, jnp.zeros((S, 32), F32)], axis=1)
    return w1, w2


def kernel(x, c, ctx, c_ctx, w_ada, b_ada, g_norm1, w_in, g_q_lat, w_uq, g_kv_lat, w_ukv, g_qn_mla, g_kn_mla,
           g_qn_nat, g_kn_nat, nat_rpb, s5_a_re, s5_a_im, s5_log_dt, s5_b_re, s5_b_im, s5_c_re, s5_c_im, s5_d,
           w_glu, w_branch, w_out, g_norm2, w_group, b_group, w_expert, b_expert, w_up, w_down):
    B, L, D = x.shape
    S = CTX + L
    assert ctx.shape[1] == CTX and D == D_MODEL and L % GRID_W == 0 and S % TM == 0
    depth = w_in.shape[0]
    rows = L // GRID_W
    cos, sin = _rope_lane_tables(L)
    rows_pad = -(-(B + 1) // 8) * 8
    c_all = jnp.concatenate([c, c_ctx[None], jnp.zeros((rows_pad - B - 1, D), c.dtype)], axis=0).astype(F32)

    xs = jnp.concatenate([ctx, x], axis=1).astype(F32)
    for l in range(depth):
        need_ctx = l < depth - 1
        ada = _ada(c_all, w_ada[l], b_ada[l])
        mods = jnp.stack([jnp.broadcast_to(ada[B].reshape(1, 6, D), (B, 6, D)), ada[:B].reshape(B, 6, D)], axis=1)

        kw1, kw2 = _head_lane_weights(g_kn_mla[l], cos, sin, 1.0)
        qw1, qw2 = _head_lane_weights(g_qn_mla[l], cos, sin, MLA_QK ** -0.5)
        gnq = jnp.tile(g_qn_nat[l].astype(F32) * NAT_DIM ** -0.5, 2).reshape(1, 128)
        gnk = jnp.tile(g_kn_nat[l].astype(F32), 2).reshape(1, 128)
        qm, km, vm, qn, kn, vn, u, gates = _front(
            xs, mods, g_norm1[l].reshape(1, D).astype(F32), _pack_w_in(w_in[l]),
            g_kv_lat[l].reshape(1, KV_LORA).astype(F32), _pack_w_ukv(w_ukv[l]),
            g_q_lat[l].reshape(1, Q_LORA).astype(F32), _pack_w_uq(w_uq[l]), kw1, kw2, qw1, qw2, gnq, gnk)

        oa = _mla(qm, km, vm, need_ctx)
        ob = _nat(qn, kn, vn, _nat_bias_table(nat_rpb[l], rows), need_ctx)
        ys = _s5(u, _s5_mats(s5_a_re[l], s5_a_im[l], s5_log_dt[l], s5_b_re[l], s5_b_im[l], s5_c_re[l], s5_c_im[l]),
                 s5_d[l])

        w_route = jnp.concatenate([w_group[l], w_expert[l], jnp.zeros((D, 128 - N_GROUPS - N_EXPERTS), F32)],
                                  axis=1).astype(F32)
        wrh, wrl = _split_bf16(w_route)
        b_route = jnp.concatenate([b_group[l], b_expert[l], jnp.zeros((128 - N_GROUPS - N_EXPERTS,), F32)]).reshape(1, 128)
        x_mid, h2, logits = _merge(
            xs, mods, oa, ob, ys, gates, w_glu[l].astype(BF16), w_branch[l].astype(BF16), w_out[l].astype(BF16), g_norm2[l].reshape(1, D).astype(F32), wrh, wrl,
            b_route.astype(F32))

        xs = _moe(x_mid, mods, h2, logits, w_up, w_down, l, 0 if need_ctx else CTX // RT)
    return xs.astype(x.dtype)
```

```python
import functools

import numpy as np
import jax
import jax.numpy as jnp
from jax import lax
from jax.experimental import pallas as pl
from jax.experimental.pallas import tpu as pltpu

F32 = jnp.float32
BF16 = jnp.bfloat16
HIGHEST = lax.Precision.HIGHEST

D_MODEL = 1024
GRID_W = 64
N_BRANCH = 3
BRANCH_W = 512
RMS_EPS = 1e-6
NEG = -1e30
HEADS = 8
MLA_NOPE = 64
MLA_ROPE = 32
MLA_QK = 96
MLA_V = 64
KV_LORA = 256
Q_LORA = 768
ROPE_THETA = 10000.0
NAT_DIM = 64
WIN_H = 8
WIN_W = 16
S5_W = 512
S5_GROUP = 16
S5_GROUPS = 32
S5_STATE = 64
N_GROUPS = 4
EXPERTS_PER_GROUP = 8
N_EXPERTS = 32
EXPERT_HIDDEN = 512
TOP_K = 2

CTX = 256
TM = 768
S5_CHUNK = 16
TC = TM // S5_CHUNK
RT = 256
MOE_BLOCK = 512
HP = 128
NAT_QROWS = 4
NAT_KROWS = NAT_QROWS + WIN_H
VMEM_LIMIT = 56 * 1024 * 1024
VMEM_LIMIT_FRONT = 60 * 1024 * 1024

P_KV = 0
P_KR = P_KV + KV_LORA
P_NK = P_KR + HP
P_NV = P_NK + 512
P_S5 = P_NV + 512
P_QL = P_S5 + 512
P_NQ = P_QL + Q_LORA
P_GATE = P_NQ + 512
P_END = P_GATE + N_BRANCH * D_MODEL

C_KV = 0
C_KR = C_KV + KV_LORA
C_NK = C_KR + MLA_ROPE
C_NV = C_NK + 512
C_S5 = C_NV + 512
C_QL = C_S5 + S5_W
C_NQ = C_QL + Q_LORA
C_GATE = C_NQ + 512


def _cparams(*sem, vmem=VMEM_LIMIT):
    return pltpu.CompilerParams(dimension_semantics=sem, vmem_limit_bytes=vmem)


def _const_spec(shape):
    nd = len(shape)
    return pl.BlockSpec(shape, lambda *_: (0,) * nd, pipeline_mode=pl.Buffered(1))


def _ada_kernel(c_ref, w_ref, b_ref, o_ref):
    c = c_ref[...]
    s = (c * jax.nn.sigmoid(c)).astype(BF16)
    o_ref[...] = jnp.dot(s, w_ref[...].astype(BF16), preferred_element_type=F32) + b_ref[...]


def _ada(c_all, w_ada, b_ada):
    rows, d = c_all.shape
    n = w_ada.shape[1]
    bn = 512
    return pl.pallas_call(
        _ada_kernel,
        out_shape=jax.ShapeDtypeStruct((rows, n), F32),
        grid=(n // bn,),
        in_specs=[pl.BlockSpec((rows, d), lambda i: (0, 0)),
                  pl.BlockSpec((d, bn), lambda i: (0, i)),
                  pl.BlockSpec((1, bn), lambda i: (0, i))],
        out_specs=pl.BlockSpec((rows, bn), lambda i: (0, i)),
        compiler_params=_cparams("arbitrary"),
        name="ada_mod",
    )(c_all, w_ada, b_ada.reshape(1, n))


def _tile_mod(mod_ref, idx, tile):
    ctx_v = mod_ref[0, 0, idx:idx + 1, :]
    lat_v = mod_ref[0, 1, idx:idx + 1, :]
    row = lax.broadcasted_iota(jnp.int32, (TM, 1), 0) + tile * TM
    return jnp.where(row < CTX, ctx_v, lat_v)


def _rms(x, g):
    return x * lax.rsqrt(jnp.mean(x * x, axis=-1, keepdims=True) + RMS_EPS) * g


def _mla_head(xh, w1, w2):
    lane = lax.broadcasted_iota(jnp.int32, xh.shape, 1)
    sq = jnp.where(lane < MLA_QK, xh * xh, 0.0)
    inv = lax.rsqrt(jnp.sum(sq, axis=-1, keepdims=True) * (1.0 / MLA_QK) + RMS_EPS)
    return inv * (xh * w1 + pltpu.roll(xh, HP - MLA_ROPE, 1) * w2)


def _pair_norm(x2, g2):
    lane = lax.broadcasted_iota(jnp.int32, x2.shape, 1)
    lo = lane < NAT_DIM
    sq = x2 * x2
    s_lo = jnp.sum(jnp.where(lo, sq, 0.0), axis=-1, keepdims=True)
    s_hi = jnp.sum(jnp.where(lo, 0.0, sq), axis=-1, keepdims=True)
    inv = lax.rsqrt(jnp.where(lo, s_lo, s_hi) * (1.0 / NAT_DIM) + RMS_EPS)
    return x2 * inv * g2


def _lane_group(shape):
    return lax.broadcasted_iota(jnp.int32, shape, 1) // S5_GROUP


def _chunk_shuffle(u_scr, ug_ref):
    grp = _lane_group((TC, 128))
    for q in range(S5_W // 128):
        for hf in range(2):
            tiles = [u_scr[q, pl.ds(8 * hf + sp, TC, stride=S5_CHUNK), :] for sp in range(8)]
            for gg in range(8):
                acc = None
                for sp in range(8):
                    sh = ((sp - gg) * S5_GROUP) % 128
                    r = tiles[sp] if sh == 0 else pltpu.roll(tiles[sp], sh, 1)
                    acc = r if acc is None else jnp.where(grp == sp, r, acc)
                ug_ref[0, 8 * q + gg, :, hf * 128:(hf + 1) * 128] = acc.astype(BF16)


def _chunk_unshuffle(yg_ref, y_scr):
    grp = _lane_group((TC, 128))
    for q in range(S5_W // 128):
        for hf in range(2):
            tiles = [yg_ref[0, 8 * q + gg, :, hf * 128:(hf + 1) * 128].astype(F32) for gg in range(8)]
            for tp in range(8):
                acc = None
                for gg in range(8):
                    sh = ((gg - tp) * S5_GROUP) % 128
                    r = tiles[gg] if sh == 0 else pltpu.roll(tiles[gg], sh, 1)
                    acc = r if acc is None else jnp.where(grp == gg, r, acc)
                y_scr[q, pl.ds(8 * hf + tp, TC, stride=S5_CHUNK), :] = acc


def _front_kernel(x_ref, mod_ref, g1_ref, win_ref, gkv_ref, wukv_ref, gq_ref, wuq_ref,
                  kw1_ref, kw2_ref, qw1_ref, qw2_ref, gnq_ref, gnk_ref,
                  qm_ref, km_ref, vm_ref, qn_ref, kn_ref, vn_ref, ug_ref, gate_ref, u_scr):
    j = pl.program_id(1)
    sh = _tile_mod(mod_ref, 0, j)
    sc = _tile_mod(mod_ref, 1, j)
    h = (_rms(x_ref[0], g1_ref[...]) * (1.0 + sc) + sh).astype(BF16)

    def proj(a, b):
        return jnp.dot(h, win_ref[:, a:b], preferred_element_type=F32)

    kvl = _rms(proj(P_KV, P_KR), gkv_ref[...]).astype(BF16)
    krp = proj(P_KR, P_NK)
    kw1 = kw1_ref[...]
    kw2 = kw2_ref[...]
    for hd in range(HEADS):
        kn = jnp.dot(kvl, wukv_ref[:, hd * HP:(hd + 1) * HP], preferred_element_type=F32)
        km_ref[0, :, hd * HP:(hd + 1) * HP] = _mla_head(kn + krp, kw1, kw2).astype(BF16)
    vm_ref[0] = jnp.dot(kvl, wukv_ref[:, HEADS * HP:], preferred_element_type=F32).astype(BF16)

    ql = _rms(proj(P_QL, P_NQ), gq_ref[...]).astype(BF16)
    qw1 = qw1_ref[...]
    qw2 = qw2_ref[...]
    for hd in range(HEADS):
        qh = jnp.dot(ql, wuq_ref[:, hd * HP:(hd + 1) * HP], preferred_element_type=F32)
        qm_ref[0, :, hd * HP:(hd + 1) * HP] = _mla_head(qh, qw1, qw2).astype(BF16)

    gnq = gnq_ref[...]
    gnk = gnk_ref[...]
    for c in range(4):
        kk = proj(P_NK + c * 128, P_NK + (c + 1) * 128)
        kn_ref[0, :, c * 128:(c + 1) * 128] = _pair_norm(kk, gnk).astype(BF16)
        qq = proj(P_NQ + c * 128, P_NQ + (c + 1) * 128)
        qn_ref[0, :, c * 128:(c + 1) * 128] = _pair_norm(qq, gnq).astype(BF16)
    vn_ref[0] = proj(P_NV, P_S5).astype(BF16)

    for q in range(S5_W // 128):
        u_scr[q] = proj(P_S5 + q * 128, P_S5 + (q + 1) * 128)
    _chunk_shuffle(u_scr, ug_ref)
    for c in range(N_BRANCH * D_MODEL // 512):
        gate_ref[0, :, c * 512:(c + 1) * 512] = jax.nn.sigmoid(
            proj(P_GATE + c * 512, P_GATE + (c + 1) * 512)).astype(BF16)


def _front(xs, mods, g1, w_in_p, gkv, wukv_p, gq, wuq_p, kw1, kw2, qw1, qw2, gnq, gnk):
    B, S, D = xs.shape
    nt = S // TM
    tok = lambda w: pl.BlockSpec((1, TM, w), lambda b, j: (b, j, 0))
    tab = pl.BlockSpec((TM, HP), lambda b, j: (j, 0))
    outs = [HEADS * HP, HEADS * HP, HEADS * HP, 512, 512, 512, None, N_BRANCH * D_MODEL]
    cw = S5_CHUNK * S5_GROUP
    ug_shape = jax.ShapeDtypeStruct((B, S5_GROUPS, S // S5_CHUNK, cw), BF16)
    ug_spec = pl.BlockSpec((1, S5_GROUPS, TC, cw), lambda b, j: (b, 0, j, 0))
    return pl.pallas_call(
        _front_kernel,
        out_shape=[ug_shape if w is None else jax.ShapeDtypeStruct((B, S, w), BF16) for w in outs],
        grid=(B, nt),
        in_specs=[tok(D),
                  pl.BlockSpec((1, 2, 6, D), lambda b, j: (b, 0, 0, 0)),
                  _const_spec((1, D)), _const_spec(w_in_p.shape), _const_spec((1, KV_LORA)),
                  _const_spec(wukv_p.shape), _const_spec((1, Q_LORA)), _const_spec(wuq_p.shape),
                  tab, tab, tab, tab, _const_spec((1, 128)), _const_spec((1, 128))],
        out_specs=[ug_spec if w is None else tok(w) for w in outs],
        scratch_shapes=[pltpu.VMEM((S5_W // 128, TM, 128), F32)],
        compiler_params=_cparams("parallel", "arbitrary", vmem=VMEM_LIMIT_FRONT),
        name="mixer_front",
    )(xs, mods, g1, w_in_p, gkv, wukv_p, gq, wuq_p, kw1, kw2, qw1, qw2, gnq, gnk)


def _nt(a, b):
    return lax.dot_general(a, b, (((1,), (1,)), ((), ())), preferred_element_type=F32)


def _softmax_pv(s, v):
    m = jnp.max(s, axis=-1, keepdims=True)
    p = jnp.exp(s - m)
    l = jnp.sum(p, axis=-1, keepdims=True)
    return jnp.dot(p.astype(BF16), v, preferred_element_type=F32) * (1.0 / l)


def _mla_kernel(q_ref, k_ref, v_ref, o_ref, *, seq, need_ctx):
    j = pl.program_id(2)

    def attend(r0, nr, nk):
        o = None
        for hh in range(2):
            sl = slice(hh * HP, (hh + 1) * HP)
            s = _nt(q_ref[0, r0:r0 + nr, sl], k_ref[0, :nk, sl])
            t = _softmax_pv(s, v_ref[0, :nk, sl])
            o = t if o is None else o + t
        o_ref[0, r0:r0 + nr, :] = o.astype(BF16)

    @pl.when(j == 0)
    def _():
        if need_ctx:
            attend(0, CTX, CTX)
        else:
            o_ref[0, 0:CTX, :] = jnp.zeros((CTX, 128), BF16)
        attend(CTX, TM - CTX, seq)

    @pl.when(j > 0)
    def _():
        attend(0, TM, seq)


def _mla(qm, km, vm, need_ctx):
    B, S, _ = qm.shape
    nt = S // TM
    return pl.pallas_call(
        functools.partial(_mla_kernel, seq=S, need_ctx=need_ctx),
        out_shape=jax.ShapeDtypeStruct((B, S, BRANCH_W), BF16),
        grid=(B, HEADS // 2, nt),
        in_specs=[pl.BlockSpec((1, TM, 2 * HP), lambda b, h, j: (b, j, h)),
                  pl.BlockSpec((1, S, 2 * HP), lambda b, h, j: (b, 0, h)),
                  pl.BlockSpec((1, S, 2 * HP), lambda b, h, j: (b, 0, h))],
        out_specs=pl.BlockSpec((1, TM, 128), lambda b, h, j: (b, j, h)),
        compiler_params=_cparams("parallel", "parallel", "arbitrary"),
        name="mla_attn",
    )(qm, km, vm)


def _nat_kernel(q_ref, k_ref, v_ref, bias_ref, o_ref, *, rows, need_ctx):
    lane = lax.broadcasted_iota(jnp.int32, (1, 128), 1)
    lo = lane < NAT_DIM
    masks = (lo, jnp.logical_not(lo))
    nq = NAT_QROWS * GRID_W
    nk = NAT_KROWS * GRID_W
    n_blocks = rows // NAT_QROWS

    def half(x, hh):
        return jnp.where(masks[hh], x, jnp.zeros_like(x))

    kc = k_ref[0, 0:CTX, :]
    vc = v_ref[0, 0:CTX, :]

    if need_ctx:
        qc = q_ref[0, 0:CTX, :]
        o = None
        for hh in range(2):
            t = _softmax_pv(_nt(half(qc, hh), kc), half(vc, hh))
            o = t if o is None else o + t
        o_ref[0, 0:CTX, :] = o.astype(BF16)
    else:
        o_ref[0, 0:CTX, :] = jnp.zeros((CTX, 128), BF16)

    def block(i, carry):
        ws = jnp.clip(i * NAT_QROWS - WIN_H // 2, 0, rows - NAT_KROWS)
        cls = jnp.where(i == 0, 0, jnp.where(i == n_blocks - 1, 2, 1))
        q0 = pl.multiple_of(CTX + i * nq, nq)
        k0 = pl.multiple_of(CTX + ws * GRID_W, GRID_W)
        q = q_ref[0, pl.ds(q0, nq), :]
        kw = k_ref[0, pl.ds(k0, nk), :]
        vw = v_ref[0, pl.ds(k0, nk), :]
        o = None
        for hh in range(2):
            qh = half(q, hh)
            s_lat = _nt(qh, kw) + bias_ref[0, hh, cls]
            s_ctx = _nt(qh, kc)
            m = jnp.maximum(jnp.max(s_lat, axis=-1, keepdims=True), jnp.max(s_ctx, axis=-1, keepdims=True))
            p_lat = jnp.exp(s_lat - m)
            p_ctx = jnp.exp(s_ctx - m)
            l = jnp.sum(p_lat, axis=-1, keepdims=True) + jnp.sum(p_ctx, axis=-1, keepdims=True)
            acc = (jnp.dot(p_lat.astype(BF16), half(vw, hh), preferred_element_type=F32)
                   + jnp.dot(p_ctx.astype(BF16), half(vc, hh), preferred_element_type=F32))
            t = acc * (1.0 / l)
            o = t if o is None else o + t
        o_ref[0, pl.ds(q0, nq), :] = o.astype(BF16)
        return carry

    lax.fori_loop(0, n_blocks, block, 0)


def _nat(qn, kn, vn, bias, need_ctx):
    B, S, _ = qn.shape
    rows = (S - CTX) // GRID_W
    seq = pl.BlockSpec((1, S, 128), lambda h, b: (b, 0, h))
    return pl.pallas_call(
        functools.partial(_nat_kernel, rows=rows, need_ctx=need_ctx),
        out_shape=jax.ShapeDtypeStruct((B, S, BRANCH_W), BF16),
        grid=(HEADS // 2, B),
        in_specs=[seq, seq, seq,
                  pl.BlockSpec((1, 2, 3, NAT_QROWS * GRID_W, NAT_KROWS * GRID_W), lambda h, b: (h, 0, 0, 0, 0))],
        out_specs=seq,
        compiler_params=_cparams("parallel", "arbitrary"),
        name="nat_attn",
    )(qn, kn, vn, bias)


def _nat_bias_table(rpb, rows):
    assert rows % NAT_QROWS == 0 and rows >= NAT_KROWS
    n_blocks = rows // NAT_QROWS
    cols = np.arange(GRID_W)
    cs = np.clip(cols - WIN_W // 2, 0, GRID_W - WIN_W)
    kcol = cols[None, :]
    qcol = cols[:, None]
    vcol = (kcol >= cs[:, None]) & (kcol < cs[:, None] + WIN_W)
    dcol = np.clip(kcol - qcol + WIN_W - 1, 0, 2 * WIN_W - 2)
    blk = np.array([0, 1, n_blocks - 1])
    ws = np.clip(blk * NAT_QROWS - WIN_H // 2, 0, rows - NAT_KROWS)
    r = blk[:, None] * NAT_QROWS + np.arange(NAT_QROWS)[None, :]
    rs = np.clip(r - WIN_H // 2, 0, rows - WIN_H)
    krow = ws[:, None, None] + np.arange(NAT_KROWS)[None, None, :]
    vrow = (krow >= rs[:, :, None]) & (krow < rs[:, :, None] + WIN_H)
    drow = np.clip(krow - r[:, :, None] + WIN_H - 1, 0, 2 * WIN_H - 2)
    r_sel = (drow[..., None] == np.arange(2 * WIN_H - 1)).astype(np.float32)
    c_sel = (dcol[..., None] == np.arange(2 * WIN_W - 1)).astype(np.float32)
    t = jnp.einsum('hrc,zajr,qkc->hzajqk', rpb.astype(F32), r_sel, c_sel, precision=HIGHEST)
    valid = vrow[:, :, :, None, None] & vcol[None, None, None]
    t = jnp.where(valid[None], t, NEG)
    t = t.transpose(0, 1, 2, 4, 3, 5).reshape(HEADS, 3, NAT_QROWS * GRID_W, NAT_KROWS * GRID_W)
    return t.reshape(HEADS // 2, 2, 3, NAT_QROWS * GRID_W, NAT_KROWS * GRID_W)


def _s5_kernel(u_ref, mt_ref, f_ref, e_ref, a_ref, d_ref, y_ref, s_scr, hin_scr, *, batch, n_ctx, n_all):
    u = u_ref[0]
    for d in range(2):
        s_scr[d] = jnp.dot(u, f_ref[d, 0], preferred_element_type=F32)

    def make_step(d):
        a1 = a_ref[d, 0, 0:1, :]
        a2 = a_ref[d, 0, 1:2, :]
        a3 = a_ref[d, 0, 2:3, :]

        def step(c, hh):
            h, hs = hh
            r0 = pl.multiple_of(c * batch, batch)
            hin_scr[d, pl.ds(r0, batch), :] = h
            s = s_scr[d, pl.ds(r0, batch), :]
            return (h * a1 + hs * a2 + s[:, :128], hs * a1 + h * a3 + s[:, 128:])
        return step

    z = jnp.zeros((batch, 128), F32)
    lax.fori_loop(0, n_all, make_step(0), (z, z))
    rstep = make_step(1)
    hh = lax.fori_loop(0, n_ctx, lambda i, c: rstep(n_ctx - 1 - i, c), (z, z))
    lax.fori_loop(0, n_all - n_ctx, lambda i, c: rstep(n_all - 1 - i, c), hh)

    y = jnp.dot(u, mt_ref[0], preferred_element_type=F32) + u.astype(F32) * d_ref[0]
    for d in range(2):
        y = y + jnp.dot(hin_scr[d].astype(BF16), e_ref[d, 0], preferred_element_type=F32)
    y_ref[0] = y.astype(BF16)


def _s5(ug, mats, d_skip):
    mt, fcat, e, a = mats
    B, _, C, w = ug.shape
    ug = ug.transpose(1, 2, 0, 3).reshape(S5_GROUPS, C * B, w)
    dg = jnp.tile(d_skip.astype(F32).reshape(S5_GROUPS, 1, S5_GROUP), (1, 1, S5_CHUNK))
    yg = pl.pallas_call(
        functools.partial(_s5_kernel, batch=B, n_ctx=CTX // S5_CHUNK, n_all=C),
        out_shape=jax.ShapeDtypeStruct((S5_GROUPS, C * B, w), BF16),
        grid=(S5_GROUPS,),
        in_specs=[pl.BlockSpec((1, C * B, w), lambda g: (g, 0, 0)),
                  pl.BlockSpec((1, w, w), lambda g: (g, 0, 0)),
                  pl.BlockSpec((2, 1, w, 256), lambda g: (0, g, 0, 0)),
                  pl.BlockSpec((2, 1, 128, w), lambda g: (0, g, 0, 0)),
                  pl.BlockSpec((2, 1, 3, 128), lambda g: (0, g, 0, 0)),
                  pl.BlockSpec((1, 1, w), lambda g: (g, 0, 0))],
        out_specs=pl.BlockSpec((1, C * B, w), lambda g: (g, 0, 0)),
        scratch_shapes=[pltpu.VMEM((2, C * B, 256), F32), pltpu.VMEM((2, C * B, 128), F32)],
        compiler_params=_cparams("parallel"),
        name="s5_scan",
    )(ug, mt, fcat, e, a, dg)
    return yg.reshape(S5_GROUPS, C, B, w).transpose(2, 0, 1, 3)


def _s5_mats(a_re, a_im, log_dt, b_re, b_im, c_re, c_im):
    L = S5_CHUNK
    lam_re = jnp.minimum(a_re.astype(F32), -1e-4)
    lam_im = a_im.astype(F32)
    dt = jnp.exp(log_dt.astype(F32))[..., None]
    mag = jnp.exp(lam_re * dt)
    ab_re = mag * jnp.cos(lam_im * dt)
    ab_im = mag * jnp.sin(lam_im * dt)
    den = lam_re * lam_re + lam_im * lam_im
    n_re = ab_re - 1.0
    f_re = (n_re * lam_re + ab_im * lam_im) / den
    f_im = (ab_im * lam_re - n_re * lam_im) / den
    br = b_re.astype(F32)
    bi = b_im.astype(F32)
    bb_re = f_re[..., None] * br - f_im[..., None] * bi
    bb_im = f_re[..., None] * bi + f_im[..., None] * br
    tau = jnp.arange(L + 1, dtype=F32)
    pm = jnp.exp((lam_re * dt)[..., None] * tau)
    ph = (lam_im * dt)[..., None] * tau
    pw_re = pm * jnp.cos(ph)
    pw_im = pm * jnp.sin(ph)
    cr = c_re.astype(F32)
    ci = c_im.astype(F32)
    pwt_re = pw_re.transpose(0, 1, 3, 2)[:, :, :, None, :]
    pwt_im = pw_im.transpose(0, 1, 3, 2)[:, :, :, None, :]
    ca_re = cr[:, :, None] * pwt_re - ci[:, :, None] * pwt_im
    ca_im = cr[:, :, None] * pwt_im + ci[:, :, None] * pwt_re
    kk = jnp.einsum('dgtnp,dgpm->dgtnm', jnp.concatenate([ca_re, -ca_im], axis=-1),
                    jnp.concatenate([bb_re, bb_im], axis=2), precision=HIGHEST)
    t = np.arange(L)
    lag_f = t[:, None] - t[None, :]
    m_f = jnp.where((lag_f >= 0)[None, :, :, None, None], kk[0][:, np.clip(lag_f, 0, L)], 0.0)
    m_r = jnp.where((lag_f <= 0)[None, :, :, None, None], kk[1][:, np.clip(-lag_f, 0, L)], 0.0)
    mt = (m_f + m_r).transpose(0, 2, 4, 1, 3).reshape(S5_GROUPS, L * S5_GROUP, L * S5_GROUP)
    pow_f = np.stack([L - 1 - t, t])
    pr = jnp.stack([pw_re[d][:, :, pow_f[d]] for d in range(2)])
    pi = jnp.stack([pw_im[d][:, :, pow_f[d]] for d in range(2)])
    fr = pr[..., None] * bb_re[:, :, :, None, :] - pi[..., None] * bb_im[:, :, :, None, :]
    fi = pr[..., None] * bb_im[:, :, :, None, :] + pi[..., None] * bb_re[:, :, :, None, :]
    fr = fr.transpose(0, 1, 3, 4, 2).reshape(2, S5_GROUPS, L * S5_GROUP, S5_STATE)
    fi = fi.transpose(0, 1, 3, 4, 2).reshape(2, S5_GROUPS, L * S5_GROUP, S5_STATE)
    fcat = jnp.concatenate([fr, fi, fi, fr], axis=-1)
    pow_e = np.stack([t + 1, L - t])
    er = jnp.stack([ca_re[d][:, pow_e[d]] for d in range(2)])
    ei = jnp.stack([ca_im[d][:, pow_e[d]] for d in range(2)])
    er = er.transpose(0, 1, 4, 2, 3).reshape(2, S5_GROUPS, S5_STATE, L * S5_GROUP)
    ei = ei.transpose(0, 1, 4, 2, 3).reshape(2, S5_GROUPS, S5_STATE, L * S5_GROUP)
    e = jnp.concatenate([er, -ei], axis=2)
    ar = pw_re[..., L]
    ai = pw_im[..., L]
    a = jnp.stack([jnp.concatenate([ar, ar], -1), jnp.concatenate([-ai, ai], -1),
                   jnp.concatenate([ai, -ai], -1)], axis=2)
    return mt.astype(BF16), fcat.astype(BF16), e.astype(BF16), a


def _split_bf16(v):
    hi = v.astype(BF16)
    return hi, (v - hi.astype(F32)).astype(BF16)


def _merge_kernel(x_ref, mod_ref, oa_ref, ob_ref, yg_ref, gate_ref, wglu_ref, wbr_ref,
                  wout_ref, g2_ref, wrh_ref, wrl_ref, br_ref, xo_ref, h2_ref, lg_ref, y_scr):
    j = pl.program_id(1)
    _chunk_unshuffle(yg_ref, y_scr)
    yc = jnp.concatenate([y_scr[q] for q in range(S5_W // 128)], axis=1)
    z = jnp.dot(jax.nn.gelu(yc).astype(BF16), wglu_ref[...], preferred_element_type=F32)
    oc = (z[:, :S5_W] * jax.nn.sigmoid(z[:, S5_W:])).astype(BF16)
    outs = (oa_ref[0], ob_ref[0], oc)
    y = None
    for i in range(N_BRANCH):
        t = gate_ref[0, :, i * D_MODEL:(i + 1) * D_MODEL].astype(F32) * jnp.dot(
            outs[i], wbr_ref[i], preferred_element_type=F32)
        y = t if y is None else y + t
    y2 = jnp.dot(y.astype(BF16), wout_ref[...], preferred_element_type=F32)
    xn = x_ref[0] + _tile_mod(mod_ref, 2, j) * y2
    xo_ref[0] = xn
    h2 = _rms(xn, g2_ref[...]) * (1.0 + _tile_mod(mod_ref, 4, j)) + _tile_mod(mod_ref, 3, j)
    hi, lo = _split_bf16(h2)
    h2_ref[0] = hi
    lg_ref[0] = (jnp.dot(hi, wrh_ref[...], preferred_element_type=F32)
                 + jnp.dot(lo, wrh_ref[...], preferred_element_type=F32)
                 + jnp.dot(hi, wrl_ref[...], preferred_element_type=F32) + br_ref[...])


def _merge(xs, mods, oa, ob, yg, gates, wglu, wbr, wout, g2, wrh, wrl, br):
    B, S, D = xs.shape
    nt = S // TM
    tok = lambda w: pl.BlockSpec((1, TM, w), lambda b, j: (b, j, 0))
    return pl.pallas_call(
        _merge_kernel,
        out_shape=[jax.ShapeDtypeStruct((B, S, D), F32), jax.ShapeDtypeStruct((B, S, D), BF16),
                   jax.ShapeDtypeStruct((B, S, 128), F32)],
        grid=(B, nt),
        in_specs=[tok(D),
                  pl.BlockSpec((1, 2, 6, D), lambda b, j: (b, 0, 0, 0)),
                  tok(BRANCH_W), tok(BRANCH_W),
                  pl.BlockSpec((1, S5_GROUPS, TC, S5_CHUNK * S5_GROUP), lambda b, j: (b, 0, j, 0)),
                  tok(N_BRANCH * D_MODEL), _const_spec(wglu.shape), _const_spec(wbr.shape),
                  _const_spec(wout.shape), _const_spec((1, D)), _const_spec(wrh.shape),
                  _const_spec(wrl.shape), _const_spec((1, 128))],
        out_specs=[tok(D), tok(D), tok(128)],
        scratch_shapes=[pltpu.VMEM((S5_W // 128, TM, 128), F32)],
        compiler_params=_cparams("parallel", "arbitrary"),
        name="merge_router",
    )(xs, mods, oa, ob, yg, gates, wglu, wbr, wout, g2, wrh, wrl, br)


def _expert_kernel(be_ref, nu_ref, x_ref, wup_ref, wdn_ref, y_ref, wup_scr, wdn_scr):
    i = pl.program_id(0)
    prev = be_ref[jnp.maximum(i - 1, 0)]

    @pl.when((i == 0) | (be_ref[i] != prev))
    def _():
        wup_scr[...] = wup_ref[0, 0].astype(BF16)
        wdn_scr[...] = wdn_ref[0, 0].astype(BF16)

    @pl.when(i < nu_ref[0])
    def _():
        a = jnp.dot(x_ref[...], wup_scr[...], preferred_element_type=F32)
        g = a[:, :EXPERT_HIDDEN]
        hid = (g * jax.nn.sigmoid(g) * a[:, EXPERT_HIDDEN:]).astype(BF16)
        y_ref[...] = jnp.dot(hid, wdn_scr[...], preferred_element_type=F32).astype(BF16)

    @pl.when(i >= nu_ref[0])
    def _():
        y_ref[...] = jnp.zeros(y_ref.shape, BF16)


def _experts(xg, blk_e, n_used, w_up, w_down, layer):
    n_rows, D = xg.shape
    n_blk = n_rows // MOE_BLOCK
    gs = pltpu.PrefetchScalarGridSpec(
        num_scalar_prefetch=2,
        grid=(n_blk,),
        in_specs=[pl.BlockSpec((MOE_BLOCK, D), lambda i, be, nu: (i, 0)),
                  pl.BlockSpec((1, 1, D, 2 * EXPERT_HIDDEN), lambda i, be, nu: (layer, be[i], 0, 0)),
                  pl.BlockSpec((1, 1, EXPERT_HIDDEN, D), lambda i, be, nu: (layer, be[i], 0, 0))],
        out_specs=pl.BlockSpec((MOE_BLOCK, D), lambda i, be, nu: (i, 0)),
        scratch_shapes=[pltpu.VMEM((D, 2 * EXPERT_HIDDEN), BF16), pltpu.VMEM((EXPERT_HIDDEN, D), BF16)],
    )
    return pl.pallas_call(
        _expert_kernel,
        out_shape=jax.ShapeDtypeStruct((n_rows, D), BF16),
        grid_spec=gs,
        compiler_params=_cparams("arbitrary"),
        name="moe_experts",
    )(blk_e, n_used, xg, w_up, w_down)


R_E0, R_E1, R_RANK0, R_RANK1, R_G0, R_G1 = 0, 1, 2, 3, 4, 5


def _router_kernel(lg_ref, tri_ref, rt_ref, cnt_ref, base_scr, *, skip_ctx):
    first = (pl.program_id(0) == 0) & (pl.program_id(1) == 0)

    @pl.when(first)
    def _():
        base_scr[...] = jnp.zeros(base_scr.shape, F32)

    lg = lg_ref[0]
    lane = lax.broadcasted_iota(jnp.int32, lg.shape, 1)
    big = jnp.int32(128)

    def first_max(v):
        m = jnp.max(v, axis=-1, keepdims=True)
        return m, jnp.min(jnp.where(v == m, lane, big), axis=-1, keepdims=True)

    gmask = lane < N_GROUPS
    gmax, grp = first_max(jnp.where(gmask, lg, -jnp.inf))
    p_grp = 1.0 / jnp.sum(jnp.where(gmask, jnp.exp(lg - gmax), 0.0), axis=-1, keepdims=True)
    e_lo = N_GROUPS + grp * EXPERTS_PER_GROUP
    el = jnp.where((lane >= e_lo) & (lane < e_lo + EXPERTS_PER_GROUP), lg, -jnp.inf)
    v1, i1 = first_max(el)
    v2, i2 = first_max(jnp.where(lane == i1, -jnp.inf, el))
    e2 = jnp.exp(v2 - v1)
    g1 = p_grp / (1.0 + e2)
    g2 = g1 * e2
    hot1 = lane == i1
    hot2 = lane == i2
    onehot = jnp.where(hot1 | hot2, 1.0, 0.0)
    if skip_ctx:
        row = lax.broadcasted_iota(jnp.int32, lg.shape, 0) + pl.program_id(1) * TM
        onehot = jnp.where(row < CTX, 0.0, onehot)
    cnt = jnp.dot(tri_ref[...], onehot.astype(BF16), preferred_element_type=F32) + base_scr[...]
    r1 = jnp.sum(jnp.where(hot1, cnt, 0.0), axis=-1, keepdims=True)
    r2 = jnp.sum(jnp.where(hot2, cnt, 0.0), axis=-1, keepdims=True)
    base_scr[...] = base_scr[...] + jnp.sum(onehot, axis=0, keepdims=True)
    cnt_ref[...] = base_scr[...]
    out = jnp.zeros(lg.shape, F32)
    for k, v in ((R_E0, (i1 - N_GROUPS).astype(F32)), (R_E1, (i2 - N_GROUPS).astype(F32)),
                 (R_RANK0, r1), (R_RANK1, r2), (R_G0, g1), (R_G1, g2)):
        out = jnp.where(lane == k, v, out)
    rt_ref[0] = out


def _router(logits, skip_ctx):
    B, S, _ = logits.shape
    tri = (np.arange(TM)[None, :] < np.arange(TM)[:, None]).astype(np.float32)
    tok = pl.BlockSpec((1, TM, 128), lambda b, j: (b, j, 0))
    return pl.pallas_call(
        functools.partial(_router_kernel, skip_ctx=skip_ctx),
        out_shape=[jax.ShapeDtypeStruct((B, S, 128), F32), jax.ShapeDtypeStruct((1, 128), F32)],
        grid=(B, S // TM),
        in_specs=[tok, _const_spec((TM, TM))],
        out_specs=[tok, pl.BlockSpec((1, 128), lambda b, j: (0, 0))],
        scratch_shapes=[pltpu.VMEM((1, 128), F32)],
        compiler_params=_cparams("arbitrary", "arbitrary"),
        name="moe_router",
    )(logits, jnp.asarray(tri, BF16))


def _combine_kernel(x_ref, mod_ref, rt_ref, y0_ref, y1_ref, o_ref):
    rt = rt_ref[0]
    y0 = y0_ref[0, 0].astype(F32)
    y1 = y1_ref[0, 0].astype(F32)
    o_ref[0] = x_ref[0] + mod_ref[0, 0, 5:6, :] * (rt[:, R_G0:R_G0 + 1] * y0 + rt[:, R_G1:R_G1 + 1] * y1)


def _combine(x_mid, mods, rt, y2, tile_off):
    B, S, D = x_mid.shape
    So = rt.shape[1]
    tok = lambda w: pl.BlockSpec((1, RT, w), lambda b, j: (b, j, 0))
    return pl.pallas_call(
        _combine_kernel,
        out_shape=jax.ShapeDtypeStruct((B, So, D), F32),
        grid=(B, So // RT),
        in_specs=[pl.BlockSpec((1, RT, D), lambda b, j: (b, j + tile_off, 0)),
                  pl.BlockSpec((1, 1, 6, D), lambda b, j: (b, jnp.minimum(j + tile_off, 1), 0, 0)),
                  tok(128), pl.BlockSpec((1, 1, RT, D), lambda b, j: (0, b, j, 0)),
                  pl.BlockSpec((1, 1, RT, D), lambda b, j: (1, b, j, 0))],
        out_specs=tok(D),
        compiler_params=_cparams("parallel", "arbitrary"),
        name="moe_combine",
    )(x_mid, mods, rt, y2, y2)


def _moe(x_mid, mods, h2, logits, w_up, w_down, layer, tile_off):
    B, S, D = x_mid.shape
    rt, cnt = _router(logits, tile_off > 0)
    rt = rt[:, tile_off * RT:]
    So = rt.shape[1]
    T = B * So
    S2 = T * TOP_K
    counts = cnt[0, N_GROUPS:N_GROUPS + N_EXPERTS].astype(jnp.int32)
    padded = (counts + MOE_BLOCK - 1) // MOE_BLOCK * MOE_BLOCK
    p_end = jnp.cumsum(padded)
    p_start = p_end - padded
    rtf = rt.reshape(T, 128)
    eid = rtf[:, R_E0:R_E1 + 1].astype(jnp.int32)
    rank = rtf[:, R_RANK0:R_RANK1 + 1].astype(jnp.int32)
    dest = (jnp.sum(jnp.where(eid[:, :, None] == jnp.arange(N_EXPERTS, dtype=jnp.int32), p_start, 0), axis=-1)
            + rank).reshape(S2)
    n_blk = -(-(S2 + N_EXPERTS * (MOE_BLOCK - 1)) // MOE_BLOCK)
    rows = (jnp.arange(B, dtype=jnp.int32)[:, None] * S + tile_off * RT
            + jnp.arange(So, dtype=jnp.int32)[None, :]).reshape(T)
    buf_row = jnp.zeros((n_blk * MOE_BLOCK,), jnp.int32).at[dest].set(
        jnp.repeat(rows, TOP_K), mode="promise_in_bounds", unique_indices=True)
    blk_start = jnp.arange(n_blk, dtype=jnp.int32) * MOE_BLOCK
    blk_e = jnp.minimum(jnp.sum((p_end[None, :] <= blk_start[:, None]).astype(jnp.int32), axis=1),
                        N_EXPERTS - 1).astype(jnp.int32)
    n_used = (p_end[-1] // MOE_BLOCK).astype(jnp.int32).reshape(1)
    xg = h2.reshape(B * S, D).at[buf_row].get(mode="promise_in_bounds")
    y = _experts(xg, blk_e, n_used, w_up, w_down, layer)
    y2 = y.at[dest.reshape(T, TOP_K).T.reshape(S2)].get(mode="promise_in_bounds").reshape(TOP_K, B, So, D)
    return _combine(x_mid, mods, rt, y2, tile_off)


def _rope_swap_index():
    i = np.arange(MLA_ROPE)
    return (i // 16) * 16 + (1 - (i % 16) // 8) * 8 + i % 8


def _pack_w_in(w_in):
    sw = _rope_swap_index()
    kr = w_in[:, C_KR:C_NK]
    z64 = jnp.zeros((w_in.shape[0], 64), w_in.dtype)
    return jnp.concatenate([w_in[:, C_KV:C_KR], z64, kr, kr[:, sw], w_in[:, C_NK:]], axis=1).astype(BF16)


def _pack_w_uq(w_uq):
    sw = _rope_swap_index()
    w = w_uq.reshape(Q_LORA, HEADS, MLA_QK)
    return jnp.concatenate([w, w[:, :, MLA_NOPE:][:, :, sw]], axis=-1).reshape(Q_LORA, HEADS * HP).astype(BF16)


def _pack_w_ukv(w_ukv):
    w = w_ukv.reshape(KV_LORA, HEADS, MLA_NOPE + MLA_V)
    z = jnp.zeros((KV_LORA, HEADS, 64), w.dtype)
    kn = jnp.concatenate([w[:, :, :MLA_NOPE], z], axis=-1)
    v = w[:, :, MLA_NOPE:]
    even = (np.arange(HEADS) % 2 == 0)[None, :, None]
    vp = jnp.concatenate([jnp.where(even, v, 0.0), jnp.where(even, 0.0, v)], axis=-1)
    return jnp.concatenate([kn.reshape(KV_LORA, HEADS * HP), vp.reshape(KV_LORA, HEADS * HP)], axis=1).astype(BF16)


def _rope_lane_tables(L):
    t = np.arange(L)
    nf = MLA_ROPE // 4
    inv = ROPE_THETA ** (-np.arange(nf, dtype=np.float64) / nf)
    ang = np.concatenate([(t // GRID_W)[:, None] * inv, (t % GRID_W)[:, None] * inv], axis=-1)
    i = np.arange(MLA_ROPE)
    col = (i // 16) * nf + i % 8
    sgn = np.where((i % 16) // 8 == 0, -1.0, 1.0)
    cos = np.concatenate([np.ones((CTX, MLA_ROPE)), np.cos(ang)[:, col]], axis=0)
    sin = np.concatenate([np.zeros((CTX, MLA_ROPE)), np.sin(ang)[:, col] * sgn], axis=0)
    return cos.astype(np.float32), sin.astype(np.float32)


def _head_lane_weights(g, cos, sin, scale):
    sw = _rope_swap_index()
    S = cos.shape[0]
    g = g.astype(F32) * scale
    w1 = jnp.concatenate([jnp.broadcast_to(g[:MLA_NOPE], (S, MLA_NOPE)), g[MLA_NOPE:] * cos,
                          jnp.zeros((S, 32), F32)], axis=1)
    w2 = jnp.concatenate([jnp.zeros((S, MLA_NOPE), F32), g[MLA_NOPE:][sw] * sin, jnp.zeros((S, 32), F32)], axis=1)
    return w1, w2


def kernel(x, c, ctx, c_ctx, w_ada, b_ada, g_norm1, w_in, g_q_lat, w_uq, g_kv_lat, w_ukv, g_qn_mla, g_kn_mla,
           g_qn_nat, g_kn_nat, nat_rpb, s5_a_re, s5_a_im, s5_log_dt, s5_b_re, s5_b_im, s5_c_re, s5_c_im, s5_d,
           w_glu, w_branch, w_out, g_norm2, w_group, b_group, w_expert, b_expert, w_up, w_down):
    B, L, D = x.shape
    S = CTX + L
    assert ctx.shape[1] == CTX and D == D_MODEL and L % GRID_W == 0 and S % TM == 0
    depth = w_in.shape[0]
    rows = L // GRID_W
    cos, sin = _rope_lane_tables(L)
    rows_pad = -(-(B + 1) // 8) * 8
    c_all = jnp.concatenate([c, c_ctx[None], jnp.zeros((rows_pad - B - 1, D), c.dtype)], axis=0).astype(F32)

    xs = jnp.concatenate([ctx, x], axis=1).astype(F32)
    for l in range(depth):
        need_ctx = l < depth - 1
        ada = _ada(c_all, w_ada[l], b_ada[l])
        mods = jnp.stack([jnp.broadcast_to(ada[B].reshape(1, 6, D), (B, 6, D)), ada[:B].reshape(B, 6, D)], axis=1)

        kw1, kw2 = _head_lane_weights(g_kn_mla[l], cos, sin, 1.0)
        qw1, qw2 = _head_lane_weights(g_qn_mla[l], cos, sin, MLA_QK ** -0.5)
        gnq = jnp.tile(g_qn_nat[l].astype(F32) * NAT_DIM ** -0.5, 2).reshape(1, 128)
        gnk = jnp.tile(g_kn_nat[l].astype(F32), 2).reshape(1, 128)
        qm, km, vm, qn, kn, vn, u, gates = _front(
            xs, mods, g_norm1[l].reshape(1, D).astype(F32), _pack_w_in(w_in[l]),
            g_kv_lat[l].reshape(1, KV_LORA).astype(F32), _pack_w_ukv(w_ukv[l]),
            g_q_lat[l].reshape(1, Q_LORA).astype(F32), _pack_w_uq(w_uq[l]), kw1, kw2, qw1, qw2, gnq, gnk)

        oa = _mla(qm, km, vm, need_ctx)
        ob = _nat(qn, kn, vn, _nat_bias_table(nat_rpb[l], rows), need_ctx)
        ys = _s5(u, _s5_mats(s5_a_re[l], s5_a_im[l], s5_log_dt[l], s5_b_re[l], s5_b_im[l], s5_c_re[l], s5_c_im[l]),
                 s5_d[l])

        w_route = jnp.concatenate([w_group[l], w_expert[l], jnp.zeros((D, 128 - N_GROUPS - N_EXPERTS), F32)],
                                  axis=1).astype(F32)
        wrh, wrl = _split_bf16(w_route)
        b_route = jnp.concatenate([b_group[l], b_expert[l], jnp.zeros((128 - N_GROUPS - N_EXPERTS,), F32)]).reshape(1, 128)
        x_mid, h2, logits = _merge(
            xs, mods, oa, ob, ys, gates, w_glu[l].astype(BF16), w_branch[l].astype(BF16), w_out[l].astype(BF16), g_norm2[l].reshape(1, D).astype(F32), wrh, wrl,
            b_route.astype(F32))

        xs = _moe(x_mid, mods, h2, logits, w_up, w_down, l, 0 if need_ctx else CTX // RT)
    return xs.astype(x.dtype)
```
